```python
import math
import jax, jax.numpy as jnp
from jax import lax
import numpy as np

D_MODEL = 1024
BATCH = 4
SEQ = 4096
DEPTH = 2
DEC_BATCH = 128
DEC_SEQ = 4
PAST_LEN = 8192
PAGE_SIZE = 128

N_EVEN = (DEPTH + 1) // 2
N_ODD = DEPTH // 2
PLE_DIM = 256
RMS_EPS = 1e-6
Q_BLOCK = 128

A_HEADS = 16
A_HEAD_DIM = 32
A_WIDTH = A_HEADS * A_HEAD_DIM
IDX_HEADS = 8
IDX_DIM = 64
IDX_SCALE = (IDX_HEADS * IDX_DIM) ** -0.5
TOPK_MAX = 256

B_HEADS = 8
B_NOPE = 64
B_ROPE = 32
B_V = 64
B_Q_LORA = 256
B_KV_LORA = 256
B_WIDTH = B_HEADS * B_V
MLA_SCALE = (B_NOPE + B_ROPE) ** -0.5
ROPE_THETA = 10000.0

C_HEADS = 16
C_KV_HEADS = 4
C_GROUP = C_HEADS // C_KV_HEADS
C_HEAD_DIM = 64
C_WIDTH = C_HEADS * C_HEAD_DIM
WINDOW = 128
C_SCALE = C_HEAD_DIM ** -0.5

REL_BUCKETS = 32
REL_MAX_DIST = 128
REL_HEADS = 16

D_FF = -(-(8 * D_MODEL) // (3 * 256)) * 256

EVEN_SPLITS = (A_WIDTH, A_WIDTH, A_WIDTH, IDX_HEADS * IDX_DIM, IDX_DIM, IDX_HEADS, B_Q_LORA, B_KV_LORA, B_ROPE)
EVEN_IN = sum(EVEN_SPLITS)
EVEN_OUT = A_WIDTH + B_WIDTH
ODD_SPLITS = (C_WIDTH, C_KV_HEADS * C_HEAD_DIM, C_KV_HEADS * C_HEAD_DIM)
ODD_IN = sum(ODD_SPLITS)

kernel_name = 'hybrid_dsa_mla_swa_decoder_step'


def rmsnorm(x, g):
    xf = x.astype(jnp.float32)
    y = xf * lax.rsqrt(jnp.mean(xf * xf, axis=-1, keepdims=True) + RMS_EPS)
    return (y * g.astype(jnp.float32)).astype(x.dtype)


def split_cols(y, sizes):
    return jnp.split(y, np.cumsum(sizes)[:-1].tolist(), axis=-1)


def rel_bucket(dist):
    n = jnp.maximum(dist, 0)
    max_exact = REL_BUCKETS // 2
    nf = jnp.maximum(n, 1).astype(jnp.float32)
    large = max_exact + (jnp.log(nf / max_exact) / math.log(REL_MAX_DIST / max_exact)
                         * (REL_BUCKETS - max_exact)).astype(jnp.int32)
    large = jnp.minimum(large, REL_BUCKETS - 1)
    return jnp.where(n < max_exact, n, large)


def rope(x, pos):
    half = x.shape[-1] // 2
    inv = ROPE_THETA ** (-jnp.arange(half, dtype=jnp.float32) / half)
    ang = pos.astype(jnp.float32)[:, None] * inv
    ang = ang.reshape(ang.shape[0], *([1] * (x.ndim - 3)), half)
    cos, sin = jnp.cos(ang), jnp.sin(ang)
    x1 = x[..., :half].astype(jnp.float32)
    x2 = x[..., half:].astype(jnp.float32)
    return jnp.concatenate([x1 * cos - x2 * sin, x1 * sin + x2 * cos], axis=-1).astype(x.dtype)


def sink_softmax(logits, sink):
    sink = sink.astype(jnp.float32)
    m = jnp.maximum(jnp.max(logits, axis=-1, keepdims=True), sink)
    ex = jnp.exp(logits - m)
    return ex / (jnp.sum(ex, axis=-1, keepdims=True) + jnp.exp(sink - m))


def swiglu(x, wg, wu, wd):
    return (jax.nn.silu(x @ wg) * (x @ wu)) @ wd


def indexer_scores(q_idx, w_idx, k_idx):
    dots = jnp.einsum('bthd,bsd->bths', q_idx, k_idx).astype(jnp.float32)
    return jnp.einsum('bth,bths->bts', w_idx.astype(jnp.float32), jax.nn.relu(dots))


def select_keys(scores, q_pos, n_keys, n_sel):
    k_pos = jnp.arange(n_keys, dtype=jnp.int32)
    causal = k_pos[None, :] <= q_pos[:, None]
    scores = jnp.where(causal[None], scores, -jnp.inf)
    _, sel = lax.top_k(scores, n_sel)
    valid = sel <= q_pos[None, :, None]
    return sel, valid


def sparse_attend(q, k_sel, v_sel, q_pos, sel, valid, rel_table):
    logits = jnp.einsum('bthd,btkhd->bthk', q, k_sel).astype(jnp.float32) * (A_HEAD_DIM ** -0.5)
    bias = rel_table[rel_bucket(q_pos[None, :, None] - sel)].astype(jnp.float32)
    logits = logits + jnp.swapaxes(bias, -1, -2)
    logits = jnp.where(valid[:, :, None, :], logits, -jnp.inf)
    probs = jax.nn.softmax(logits, axis=-1)
    return jnp.einsum('bthk,btkhd->bthd', probs.astype(v_sel.dtype), v_sel)


def mla_attend(q_lat, q_rope, segments):
    logits = []
    for ckv, kr, mask in segments:
        lg = (jnp.einsum('bthc,blc->bhtl', q_lat, ckv)
              + jnp.einsum('bthr,blr->bhtl', q_rope, kr)).astype(jnp.float32) * MLA_SCALE
        logits.append(jnp.where(mask, lg, -jnp.inf))
    probs = jax.nn.softmax(jnp.concatenate(logits, axis=-1), axis=-1)
    outs = []
    start = 0
    for ckv, _, _ in segments:
        n = ckv.shape[1]
        outs.append(jnp.einsum('bhtl,blc->bthc', probs[..., start:start + n].astype(ckv.dtype), ckv))
        start += n
    out = outs[0]
    for o in outs[1:]:
        out = out + o
    return out


def even_project(hn, pos, w_in, g_bq, w_buq, g_bkv, w_buk):
    B, T, _ = hn.shape
    qa, ka, va, qi, ki, wi, cq, ckv, kr = split_cols(hn @ w_in, EVEN_SPLITS)
    qa = qa.reshape(B, T, A_HEADS, A_HEAD_DIM)
    ka = ka.reshape(B, T, A_HEADS, A_HEAD_DIM)
    va = va.reshape(B, T, A_HEADS, A_HEAD_DIM)
    qi = qi.reshape(B, T, IDX_HEADS, IDX_DIM)
    wi = wi * IDX_SCALE
    q = (rmsnorm(cq, g_bq) @ w_buq).reshape(B, T, B_HEADS, B_NOPE + B_ROPE)
    q_rope = rope(q[..., B_NOPE:], pos)
    q_lat = jnp.einsum('bthn,chn->bthc', q[..., :B_NOPE], w_buk)
    ckv = rmsnorm(ckv, g_bkv)
    kr = rope(kr, pos)
    return qa, ka, va, qi, ki, wi, q_lat, q_rope, ckv, kr


def even_merge(oa, ob_lat, w_buv, w_out):
    B, T = oa.shape[:2]
    ob = jnp.einsum('bthc,chv->bthv', ob_lat, w_buv)
    return jnp.concatenate([oa.reshape(B, T, A_WIDTH), ob.reshape(B, T, B_WIDTH)], axis=-1) @ w_out


def even_mixer_prompt(hn, w_in, w_out, g_bq, w_buq, g_bkv, w_buk, w_buv, rel_table):
    B, S, _ = hn.shape
    pos = jnp.arange(S, dtype=jnp.int32)
    qa, ka, va, qi, ki, wi, q_lat, q_rope, ckv, kr = even_project(hn, pos, w_in, g_bq, w_buq, g_bkv, w_buk)
    n_sel = min(TOPK_MAX, S // 4)
    nb = S // Q_BLOCK

    def to_blocks(a):
        return jnp.swapaxes(a.reshape(B, nb, Q_BLOCK, *a.shape[2:]), 0, 1)

    def from_blocks(a):
        return jnp.swapaxes(a, 0, 1).reshape(B, S, *a.shape[3:])

    take_rows = jax.vmap(lambda rows, idx: rows[idx])
    pos_blocks = pos.reshape(nb, Q_BLOCK)

    def a_block(args):
        q_b, qi_b, wi_b, pos_b = args
        sel, valid = select_keys(indexer_scores(qi_b, wi_b, ki), pos_b, S, n_sel)
        return sparse_attend(q_b, take_rows(ka, sel), take_rows(va, sel), pos_b, sel, valid, rel_table)

    oa = from_blocks(lax.map(a_block, (to_blocks(qa), to_blocks(qi), to_blocks(wi), pos_blocks)))

    def b_block(args):
        ql_b, qr_b, pos_b = args
        return mla_attend(ql_b, qr_b, [(ckv, kr, pos[None, :] <= pos_b[:, None])])

    ob = from_blocks(lax.map(b_block, (to_blocks(q_lat), to_blocks(q_rope), pos_blocks)))
    return even_merge(oa, ob, w_buv, w_out), (ka, va, ki, ckv, kr)


def even_mixer_sample(hn, cache_a_k, cache_a_v, cache_a_idx, cache_b_ckv, cache_b_krope, page_table, e,
                      w_in, w_out, g_bq, w_buq, g_bkv, w_buk, w_buv, rel_table):
    Bd, T, _ = hn.shape
    pos = PAST_LEN + jnp.arange(T, dtype=jnp.int32)
    n_keys = PAST_LEN + T
    qa, ka, va, qi, ki, wi, q_lat, q_rope, ckv, kr = even_project(hn, pos, w_in, g_bq, w_buq, g_bkv, w_buk)
    layer_pages = jnp.full(page_table.shape, e, jnp.int32)

    def past_rows(pool):
        g = pool[layer_pages, page_table]
        return g.reshape(Bd, PAST_LEN, *g.shape[3:])

    ki_all = jnp.concatenate([past_rows(cache_a_idx), ki], axis=1)
    n_sel = min(TOPK_MAX, n_keys // 4)
    sel, valid = select_keys(indexer_scores(qi, wi, ki_all), pos, n_keys, n_sel)
    in_past = sel < PAST_LEN
    sp = jnp.minimum(sel, PAST_LEN - 1)
    phys = jax.vmap(lambda pt, lp: pt[lp])(page_table, sp // PAGE_SIZE)
    off = sp % PAGE_SIZE
    layer_sel = jnp.full(sel.shape, e, jnp.int32)
    sn = jnp.clip(sel - PAST_LEN, 0, T - 1)
    take_rows = jax.vmap(lambda rows, idx: rows[idx])

    def gather_sel(pool, new):
        return jnp.where(in_past[..., None, None], pool[layer_sel, phys, off], take_rows(new, sn))

    oa = sparse_attend(qa, gather_sel(cache_a_k, ka), gather_sel(cache_a_v, va), pos, sel, valid, rel_table)

    past_mask = jnp.ones((T, PAST_LEN), dtype=bool)
    t_idx = jnp.arange(T)
    new_mask = t_idx[None, :] <= t_idx[:, None]
    ob = mla_attend(q_lat, q_rope, [(past_rows(cache_b_ckv), past_rows(cache_b_krope), past_mask),
                                    (ckv, kr, new_mask)])
    return even_merge(oa, ob, w_buv, w_out), (ka, va, ki, ckv, kr)


def odd_project(hn, w_in):
    B, T, _ = hn.shape
    q, k, v = split_cols(hn @ w_in, ODD_SPLITS)
    return (q.reshape(B, T, C_KV_HEADS, C_GROUP, C_HEAD_DIM),
            k.reshape(B, T, C_KV_HEADS, C_HEAD_DIM),
            v.reshape(B, T, C_KV_HEADS, C_HEAD_DIM))


def odd_mixer_prompt(hn, w_in, w_out, sinks, rel_table):
    B, S, _ = hn.shape
    q, k, v = odd_project(hn, w_in)
    nb = S // WINDOW
    qb = q.reshape(B, nb, WINDOW, C_KV_HEADS, C_GROUP, C_HEAD_DIM)

    def band(a):
        ab = a.reshape(B, nb, WINDOW, C_KV_HEADS, C_HEAD_DIM)
        prev = jnp.concatenate([jnp.zeros_like(ab[:, :1]), ab[:, :-1]], axis=1)
        return jnp.concatenate([prev, ab], axis=2)

    kk, vv = band(k), band(v)
    i = jnp.arange(WINDOW)
    j = jnp.arange(2 * WINDOW)
    dist = WINDOW + i[:, None] - j[None, :]
    in_band = (dist >= 0) & (dist <= WINDOW)
    first_blk = (jnp.arange(nb) == 0)[:, None, None]
    key_ok = ~(first_blk & (j[None, None, :] < WINDOW))
    mask = in_band[None] & key_ok
    bias = rel_table[rel_bucket(dist)].astype(jnp.float32)
    bias = bias.reshape(WINDOW, 2 * WINDOW, C_KV_HEADS, C_GROUP).transpose(2, 3, 0, 1)
    logits = jnp.einsum('bnqkgd,bnskd->bnkgqs', qb, kk).astype(jnp.float32) * C_SCALE + bias
    logits = jnp.where(mask[None, :, None, None], logits, -jnp.inf)
    probs = sink_softmax(logits, sinks.reshape(C_KV_HEADS, C_GROUP)[None, None, :, :, None, None])
    out = jnp.einsum('bnkgqs,bnskd->bnqkgd', probs.astype(vv.dtype), vv).reshape(B, S, C_WIDTH)
    wp = min(WINDOW, S)
    return out @ w_out, (k[:, S - wp:], v[:, S - wp:])


def odd_mixer_sample(hn, buf_k, buf_v, w_in, w_out, sinks, rel_table):
    Bd, T, _ = hn.shape
    q, k, v = odd_project(hn, w_in)
    wb = buf_k.shape[1]
    kk = jnp.concatenate([buf_k, k], axis=1)
    vv = jnp.concatenate([buf_v, v], axis=1)
    key_pos = PAST_LEN - wb + jnp.arange(wb + T, dtype=jnp.int32)
    q_pos = PAST_LEN + jnp.arange(T, dtype=jnp.int32)
    dist = q_pos[:, None] - key_pos[None, :]
    mask = (dist >= 0) & (dist <= WINDOW)
    bias = rel_table[rel_bucket(dist)].astype(jnp.float32)
    bias = bias.reshape(T, wb + T, C_KV_HEADS, C_GROUP).transpose(2, 3, 0, 1)
    logits = jnp.einsum('btkgd,bskd->bkgts', q, kk).astype(jnp.float32) * C_SCALE + bias
    logits = jnp.where(mask, logits, -jnp.inf)
    probs = sink_softmax(logits, sinks.reshape(C_KV_HEADS, C_GROUP)[None, :, :, None, None])
    out = jnp.einsum('bkgts,bskd->btkgd', probs.astype(vv.dtype), vv).reshape(Bd, T, C_WIDTH)
    return out @ w_out, (kk[:, T:], vv[:, T:])


def layer_tail(h, p_i, i, g_ffn, w_ffn_gate, w_ffn_up, w_ffn_down, g_ple, w_ple_gate, w_ple_proj):
    h = h + swiglu(rmsnorm(h, g_ffn[i]), w_ffn_gate[i], w_ffn_up[i], w_ffn_down[i])
    gate = jax.nn.sigmoid(rmsnorm(h, g_ple[i]) @ w_ple_gate[i])
    return h + gate * (p_i @ w_ple_proj[i])


def setup_inputs(seed: int = 0) -> dict:
    key = jax.random.key(seed)
    ks = iter(jax.random.split(key, 40))
    f32 = jnp.float32

    def nrm(shape, scale=1.0):
        a = jax.random.normal(next(ks), shape, f32)
        return a if scale == 1.0 else a * scale

    def gain(shape):
        return 1.0 + 0.05 * jax.random.normal(next(ks), shape, f32)

    n_pages = PAST_LEN // PAGE_SIZE
    n_used = DEC_BATCH * n_pages
    n_pool = n_used + -(-n_used // 4)
    wb = min(WINDOW, PAST_LEN)
    x_prompt = nrm((BATCH, SEQ, D_MODEL))
    x_sample = nrm((DEC_BATCH, DEC_SEQ, D_MODEL))
    cache_a_k = nrm((N_EVEN, n_pool, PAGE_SIZE, A_HEADS, A_HEAD_DIM))
    cache_a_v = nrm((N_EVEN, n_pool, PAGE_SIZE, A_HEADS, A_HEAD_DIM))
    cache_a_idx = nrm((N_EVEN, n_pool, PAGE_SIZE, IDX_DIM))
    cache_b_ckv = nrm((N_EVEN, n_pool, PAGE_SIZE, B_KV_LORA))
    cache_b_krope = nrm((N_EVEN, n_pool, PAGE_SIZE, B_ROPE))
    state_c_k = nrm((N_ODD, DEC_BATCH, wb, C_KV_HEADS, C_HEAD_DIM))
    state_c_v = nrm((N_ODD, DEC_BATCH, wb, C_KV_HEADS, C_HEAD_DIM))
    page_table = jax.random.permutation(next(ks), n_pool)[:n_used].reshape(DEC_BATCH, n_pages).astype(jnp.int32)
    p_prompt = nrm((DEPTH, BATCH, SEQ, PLE_DIM))
    p_sample = nrm((DEPTH, DEC_BATCH, DEC_SEQ, PLE_DIM))
    return {
        'x_prompt': x_prompt,
        'x_sample': x_sample,
        'cache_a_k': cache_a_k,
        'cache_a_v': cache_a_v,
        'cache_a_idx': cache_a_idx,
        'cache_b_ckv': cache_b_ckv,
        'cache_b_krope': cache_b_krope,
        'state_c_k': state_c_k,
        'state_c_v': state_c_v,
        'page_table': page_table,
        'p_prompt': p_prompt,
        'p_sample': p_sample,
        'rel_table': nrm((REL_BUCKETS, REL_HEADS), 0.5),
        'w_in_even': nrm((N_EVEN, D_MODEL, EVEN_IN), D_MODEL ** -0.5),
        'w_out_even': nrm((N_EVEN, EVEN_OUT, D_MODEL), EVEN_OUT ** -0.5),
        'g_bq': gain((N_EVEN, B_Q_LORA)),
        'w_buq': nrm((N_EVEN, B_Q_LORA, B_HEADS * (B_NOPE + B_ROPE)), B_Q_LORA ** -0.5),
        'g_bkv': gain((N_EVEN, B_KV_LORA)),
        'w_buk': nrm((N_EVEN, B_KV_LORA, B_HEADS, B_NOPE), B_KV_LORA ** -0.5),
        'w_buv': nrm((N_EVEN, B_KV_LORA, B_HEADS, B_V), B_KV_LORA ** -0.5),
        'w_in_odd': nrm((N_ODD, D_MODEL, ODD_IN), D_MODEL ** -0.5),
        'w_out_odd': nrm((N_ODD, C_WIDTH, D_MODEL), C_WIDTH ** -0.5),
        'c_sinks': nrm((N_ODD, C_HEADS)),
        'g_mix': gain((DEPTH, D_MODEL)),
        'g_ffn': gain((DEPTH, D_MODEL)),
        'w_ffn_gate': nrm((DEPTH, D_MODEL, D_FF), D_MODEL ** -0.5),
        'w_ffn_up': nrm((DEPTH, D_MODEL, D_FF), D_MODEL ** -0.5),
        'w_ffn_down': nrm((DEPTH, D_FF, D_MODEL), D_FF ** -0.5),
        'g_ple': gain((DEPTH, D_MODEL)),
        'w_ple_gate': nrm((DEPTH, D_MODEL, D_MODEL), D_MODEL ** -0.5),
        'w_ple_proj': nrm((DEPTH, PLE_DIM, D_MODEL), PLE_DIM ** -0.5),
        'g_final': gain((D_MODEL,)),
    }


def reference(x_prompt, x_sample, cache_a_k, cache_a_v, cache_a_idx, cache_b_ckv, cache_b_krope,
              state_c_k, state_c_v, page_table, p_prompt, p_sample, rel_table,
              w_in_even, w_out_even, g_bq, w_buq, g_bkv, w_buk, w_buv,
              w_in_odd, w_out_odd, c_sinks, g_mix, g_ffn, w_ffn_gate, w_ffn_up, w_ffn_down,
              g_ple, w_ple_gate, w_ple_proj, g_final):
    hp, hs = x_prompt, x_sample
    pa_k, pa_v, pa_i, pb_c, pb_r, pc_k, pc_v = [], [], [], [], [], [], []
    sa_k, sa_v, sa_i, sb_c, sb_r, sc_k, sc_v = [], [], [], [], [], [], []
    for i in range(DEPTH):
        li = i // 2
        if i % 2 == 0:
            mp, (ka, va, ki, ckv, kr) = even_mixer_prompt(
                rmsnorm(hp, g_mix[i]), w_in_even[li], w_out_even[li], g_bq[li], w_buq[li],
                g_bkv[li], w_buk[li], w_buv[li], rel_table)
            pa_k.append(ka); pa_v.append(va); pa_i.append(ki); pb_c.append(ckv); pb_r.append(kr)
            ms, (ka, va, ki, ckv, kr) = even_mixer_sample(
                rmsnorm(hs, g_mix[i]), cache_a_k, cache_a_v, cache_a_idx, cache_b_ckv, cache_b_krope,
                page_table, li, w_in_even[li], w_out_even[li], g_bq[li], w_buq[li],
                g_bkv[li], w_buk[li], w_buv[li], rel_table)
            sa_k.append(ka); sa_v.append(va); sa_i.append(ki); sb_c.append(ckv); sb_r.append(kr)
        else:
            mp, (ck, cv) = odd_mixer_prompt(rmsnorm(hp, g_mix[i]), w_in_odd[li], w_out_odd[li],
                                            c_sinks[li], rel_table)
            pc_k.append(ck); pc_v.append(cv)
            ms, (ck, cv) = odd_mixer_sample(rmsnorm(hs, g_mix[i]), state_c_k[li], state_c_v[li],
                                            w_in_odd[li], w_out_odd[li], c_sinks[li], rel_table)
            sc_k.append(ck); sc_v.append(cv)
        hp = layer_tail(hp + mp, p_prompt[i], i, g_ffn, w_ffn_gate, w_ffn_up, w_ffn_down,
                        g_ple, w_ple_gate, w_ple_proj)
        hs = layer_tail(hs + ms, p_sample[i], i, g_ffn, w_ffn_gate, w_ffn_up, w_ffn_down,
                        g_ple, w_ple_gate, w_ple_proj)
    y_prompt = rmsnorm(hp, g_final)
    y_sample = rmsnorm(hs, g_final)
    return (y_prompt, y_sample,
            jnp.stack(pa_k), jnp.stack(pa_v), jnp.stack(pa_i), jnp.stack(pb_c), jnp.stack(pb_r),
            jnp.stack(pc_k), jnp.stack(pc_v),
            jnp.stack(sa_k), jnp.stack(sa_v), jnp.stack(sa_i), jnp.stack(sb_c), jnp.stack(sb_r),
            jnp.stack(sc_k), jnp.stack(sc_v))
```

```python
import functools
import math

import numpy as np
import jax
import jax.numpy as jnp
from jax import lax
from jax.experimental import pallas as pl
from jax.experimental.pallas import tpu as pltpu

F32 = jnp.float32
BF16 = jnp.bfloat16

RMS_EPS = 1e-6
A_HEADS, A_HEAD_DIM = 16, 32
A_WIDTH = A_HEADS * A_HEAD_DIM
IDX_HEADS, IDX_DIM = 8, 64
IDX_WIDTH = IDX_HEADS * IDX_DIM
IDX_SCALE = IDX_WIDTH ** -0.5
TOPK_MAX = 256
B_HEADS, B_NOPE, B_ROPE, B_V = 8, 64, 32, 64
B_Q_LORA, B_KV_LORA = 256, 256
B_WIDTH = B_HEADS * B_V
MLA_SCALE = (B_NOPE + B_ROPE) ** -0.5
ROPE_THETA = 10000.0
C_HEADS, C_KV_HEADS, C_HEAD_DIM = 16, 4, 64
C_GROUP = C_HEADS // C_KV_HEADS
C_WIDTH = C_HEADS * C_HEAD_DIM
C_KV_WIDTH = C_KV_HEADS * C_HEAD_DIM
WINDOW = 128
C_SCALE = C_HEAD_DIM ** -0.5
A_SCALE = A_HEAD_DIM ** -0.5
REL_BUCKETS, REL_MAX_DIST = 32, 128
PAGE = 128
QB = 128
LANE = 128
MLA_KB = 512
ROW_TILE = 256
VMEM_LIMIT = 56 * 1024 * 1024

INT_MIN = np.int32(-2 ** 31)
KEY_NEG = np.int32(-2139095041)
NEG_INF = float("-inf")

_NT = (((1,), (1,)), ((), ()))


def _dot(a, b):
    return jnp.dot(a, b, preferred_element_type=F32)


def _dot_nt(a, b):
    return lax.dot_general(a, b, _NT, preferred_element_type=F32)


def _rms(x, g):
    ms = jnp.mean(x * x, axis=-1, keepdims=True)
    return x * lax.rsqrt(ms + RMS_EPS) * g


def _const_spec(shape):
    nd = len(shape)
    return pl.BlockSpec(shape, lambda *_: (0,) * nd, pipeline_mode=pl.Buffered(1))


def _params(sem):
    return pltpu.CompilerParams(dimension_semantics=sem, vmem_limit_bytes=VMEM_LIMIT)


def _rope_apply(x, cos, sin_signed):
    n = x.shape[1]
    lane = lax.broadcasted_iota(jnp.int32, x.shape, 1)
    partner = jnp.where((lane % B_ROPE) < (B_ROPE // 2),
                        pltpu.roll(x, n - B_ROPE // 2, 1), pltpu.roll(x, B_ROPE // 2, 1))
    return x * cos + partner * sin_signed


def _even_proj_kernel(x_ref, g_ref, w_ref, gq_ref, wn_ref, wr_ref, wk_ref, gkv_ref, cos_ref, sin_ref,
                      qa_o, ka_o, kab_o, va_o, vab_o, qi_o, ki_o, ki2_o, wi_o, ql_o, qr_o,
                      ckv_o, ckvb_o, kr_o, krr_o):
    hn = _rms(x_ref[...], g_ref[...]).astype(BF16)
    aw = A_WIDTH
    qa_o[...] = _dot(hn, w_ref[:, 0:aw]).astype(BF16)
    ka = _dot(hn, w_ref[:, aw:2 * aw])
    ka_o[...] = ka
    kab_o[...] = ka.astype(BF16)
    va = _dot(hn, w_ref[:, 2 * aw:3 * aw])
    va_o[...] = va
    vab_o[...] = va.astype(BF16)
    qi_o[...] = _dot(hn, w_ref[:, 3 * aw:4 * aw]).astype(BF16)
    c0 = 4 * aw
    cq = _dot(hn, w_ref[:, c0:c0 + B_Q_LORA])
    ckv = _dot(hn, w_ref[:, c0 + B_Q_LORA:c0 + B_Q_LORA + B_KV_LORA])
    c1 = c0 + B_Q_LORA + B_KV_LORA
    ki2 = _dot(hn, w_ref[:, c1:c1 + 2 * IDX_DIM])
    krr = _dot(hn, w_ref[:, c1 + LANE:c1 + LANE + 256])
    wi = _dot(hn, w_ref[:, c1 + LANE + 256:c1 + 2 * LANE + 256])
    ki_o[...] = ki2[:, 0:IDX_DIM]
    ki2_o[...] = ki2.astype(BF16)
    wi_o[...] = wi[:, 0:IDX_HEADS] * IDX_SCALE
    cos = cos_ref[...]
    sin = sin_ref[...]
    krr = _rope_apply(krr, cos, sin)
    kr_o[...] = krr[:, 0:B_ROPE]
    krr_o[...] = krr.astype(BF16)
    ckvn = _rms(ckv, gkv_ref[...])
    ckv_o[...] = ckvn
    ckvb_o[...] = ckvn.astype(BF16)
    cqn = _rms(cq, gq_ref[...]).astype(BF16)
    qn = _dot(cqn, wn_ref[...]).astype(BF16)
    qr = _dot(cqn, wr_ref[...])
    qr_o[...] = _rope_apply(qr, cos, sin).astype(BF16)
    for p in range(B_HEADS // 2):
        ql_o[:, p * 512:(p + 1) * 512] = _dot(qn[:, p * LANE:(p + 1) * LANE], wk_ref[p]).astype(BF16)


def _even_proj(x, g, wts, cos, sin):
    n, d = x.shape
    tm = min(ROW_TILE, n)
    nblk = n // tm
    tblk = cos.shape[0] // tm
    row = lambda w: pl.BlockSpec((tm, w), lambda i: (i, 0))
    tab = pl.BlockSpec((tm, 256), lambda i: (i % tblk, 0))
    outs = [(A_WIDTH, BF16), (A_WIDTH, F32), (A_WIDTH, BF16), (A_WIDTH, F32), (A_WIDTH, BF16),
            (IDX_WIDTH, BF16), (IDX_DIM, F32), (2 * IDX_DIM, BF16), (IDX_HEADS, F32),
            (B_HEADS * B_KV_LORA, BF16), (B_HEADS * B_ROPE, BF16),
            (B_KV_LORA, F32), (B_KV_LORA, BF16), (B_ROPE, F32), (B_HEADS * B_ROPE, BF16)]
    res = pl.pallas_call(
        _even_proj_kernel,
        grid=(nblk,),
        in_specs=[row(d), _const_spec((1, d)), _const_spec(wts["w_in"].shape),
                  _const_spec((1, B_Q_LORA)), _const_spec(wts["w_qn"].shape), _const_spec(wts["w_qr"].shape),
                  _const_spec(wts["w_bukbd"].shape), _const_spec((1, B_KV_LORA)), tab, tab],
        out_specs=[row(w) for w, _ in outs],
        out_shape=[jax.ShapeDtypeStruct((n, w), dt) for w, dt in outs],
        compiler_params=_params(("parallel",)),
        name="even_proj",
    )(x, g, wts["w_in"], wts["g_bq"], wts["w_qn"], wts["w_qr"], wts["w_bukbd"], wts["g_bkv"], cos, sin)
    names = ["qa", "ka", "ka_bf", "va", "va_bf", "qi", "ki", "ki2", "wi", "q_lat", "q_rope",
             "ckv", "ckv_bf", "kr", "kr_rep"]
    return dict(zip(names, res))


def _attn_a_prompt_kernel(qi_ref, wi_ref, qa_ref, ki2_ref, ka_ref, va_ref, bias_ref, o_ref,
                          qst, wb, keybuf, maskb, lbuf, *, n_sel, idx_bits):
    i = pl.program_id(1)
    nk2 = (i + 2) // 2
    lane1 = lax.broadcasted_iota(jnp.int32, (QB, LANE), 1)
    for h in range(IDX_HEADS):
        blk = qi_ref[0, :, (h // 2) * LANE:(h // 2 + 1) * LANE]
        qst[h * QB:(h + 1) * QB, :] = jnp.where((lane1 // IDX_DIM) == (h % 2), blk, jnp.zeros_like(blk))
        wb[h] = jnp.broadcast_to(wi_ref[0, :, h:h + 1], (QB, 2 * LANE))
    row2 = lax.broadcasted_iota(jnp.int32, (QB, 2 * LANE), 0)
    col2 = lax.broadcasted_iota(jnp.int32, (QB, 2 * LANE), 1)

    def score_chunk(c, carry):
        kc = ki2_ref[0, pl.ds(pl.multiple_of(c * 256, 256), 256), :]
        d = _dot_nt(qst[...], kc)
        acc = wb[0] * jnp.maximum(d[0:QB, :], 0.0)
        for h in range(1, IDX_HEADS):
            acc = acc + wb[h] * jnp.maximum(d[h * QB:(h + 1) * QB, :], 0.0)
        acc = jnp.where(acc == 0.0, 0.0, acc)
        bits = pltpu.bitcast(acc, jnp.int32)
        key = jnp.where(bits < 0, bits ^ jnp.int32(0x7FFFFFFF), bits)
        valid = (c * 256 + col2) <= (i * QB + row2)
        key = jnp.where(valid, key, KEY_NEG)
        keybuf[2 * c] = key[:, 0:LANE]
        keybuf[2 * c + 1] = key[:, LANE:2 * LANE]
        return carry

    lax.fori_loop(0, nk2, score_chunk, 0)

    def count(pred):
        def body(c, cnt):
            return cnt + jnp.where(pred(keybuf[c], c), 1.0, 0.0)
        cnt = lax.fori_loop(0, 2 * nk2, body, jnp.zeros((QB, LANE), F32))
        return jnp.sum(cnt, axis=1, keepdims=True)

    def bit_step(b, t_u):
        cand_u = t_u | (jnp.int32(1) << (31 - b))
        cb = jnp.broadcast_to(cand_u ^ INT_MIN, (QB, LANE))
        tot = count(lambda kk, c: kk >= cb)
        return jnp.where(tot >= n_sel, cand_u, t_u)

    t_u = lax.fori_loop(0, 32, bit_step, jnp.zeros((QB, 1), jnp.int32))
    t_s = t_u ^ INT_MIN
    tb = jnp.broadcast_to(t_s, (QB, LANE))
    cge = count(lambda kk, c: kk >= tb)
    tie_rows = jnp.logical_and(t_s > KEY_NEG, cge > n_sel)
    any_tie = jnp.max(jnp.where(tie_rows, 1.0, 0.0)) > 0.0

    @pl.when(jnp.logical_not(any_tie))
    def _():
        def body(c, carry):
            kk = keybuf[c]
            maskb[c] = jnp.where(jnp.logical_and(kk >= tb, kk > KEY_NEG), 0.0, NEG_INF)
            return carry
        lax.fori_loop(0, 2 * nk2, body, 0)

    @pl.when(any_tie)
    def _():
        need = n_sel - count(lambda kk, c: kk > tb)

        def jstep(b, j_u):
            cand = j_u | (jnp.int32(1) << (idx_bits - 1 - b))
            cb = jnp.broadcast_to(cand, (QB, LANE))
            f = count(lambda kk, c: jnp.logical_and(kk == tb, (c * LANE + lane1) < cb))
            return jnp.where(f < need, cand, j_u)

        j_u = lax.fori_loop(0, idx_bits, jstep, jnp.zeros((QB, 1), jnp.int32))
        jb = jnp.broadcast_to(j_u, (QB, LANE))

        def body(c, carry):
            kk = keybuf[c]
            keep = jnp.logical_or(kk > tb, jnp.logical_and(kk == tb, (c * LANE + lane1) <= jb))
            maskb[c] = jnp.where(jnp.logical_and(keep, kk > KEY_NEG), 0.0, NEG_INF)
            return carry
        lax.fori_loop(0, 2 * nk2, body, 0)

    lbuf[0] = jnp.full((QB, LANE), NEG_INF, F32)
    lane2 = col2 // A_HEAD_DIM
    for g in range(A_HEADS // 8):
        qg = qa_ref[0, :, g * 256:(g + 1) * 256]
        out_g = jnp.zeros((QB, 256), F32)
        for hh in range(8):
            qm = jnp.where(lane2 == hh, qg, jnp.zeros_like(qg))

            def p1(c, m):
                kc = ka_ref[0, pl.ds(pl.multiple_of(c * 256, 256), 256), g * 256:(g + 1) * 256]
                l = _dot_nt(qm, kc) * A_SCALE
                l0 = l[:, 0:LANE] + maskb[2 * c]
                l1 = l[:, LANE:2 * LANE] + maskb[2 * c + 1]
                lbuf[2 * c + 1] = l0
                lbuf[2 * c + 2] = l1
                return jnp.maximum(m, jnp.maximum(l0, l1))

            m = lax.fori_loop(0, nk2, p1, jnp.full((QB, LANE), NEG_INF, F32))
            bh = bias_ref[g * 8 + hh]
            w0 = lbuf[i] + bh[:, 0:LANE]
            w1 = lbuf[i + 1] + bh[:, LANE:2 * LANE]
            lbuf[i] = w0
            lbuf[i + 1] = w1
            m = jnp.maximum(m, jnp.maximum(w0, w1))
            mrow = jnp.max(m, axis=1, keepdims=True)

            def p2(c, carry):
                s, acc = carry
                p0 = jnp.exp(lbuf[2 * c + 1] - mrow)
                p1_ = jnp.exp(lbuf[2 * c + 2] - mrow)
                vc = va_ref[0, pl.ds(pl.multiple_of(c * 256, 256), 256), g * 256:(g + 1) * 256]
                pp = jnp.concatenate([p0, p1_], axis=1).astype(BF16)
                return s + p0 + p1_, acc + _dot(pp, vc)

            s, acc = lax.fori_loop(0, nk2, p2, (jnp.zeros((QB, LANE), F32), jnp.zeros((QB, 256), F32)))
            den = jnp.sum(s, axis=1, keepdims=True)
            out_g = jnp.where(lane2 == hh, acc / den, out_g)
        o_ref[0, :, g * 256:(g + 1) * 256] = out_g.astype(BF16)


def _attn_a_prompt(qi, wi, qa, ki2, ka, va, bias_win):
    b, s, _ = qa.shape
    nq = s // QB
    n_sel = min(TOPK_MAX, s // 4)
    idx_bits = int(math.log2(s)) + 1
    kern = functools.partial(_attn_a_prompt_kernel, n_sel=n_sel, idx_bits=idx_bits)
    qblk = lambda w: pl.BlockSpec((1, QB, w), lambda bi, i: (bi, i, 0))
    full = lambda w: pl.BlockSpec((1, s, w), lambda bi, i: (bi, 0, 0))
    return pl.pallas_call(
        kern,
        grid=(b, nq),
        in_specs=[qblk(IDX_WIDTH), qblk(IDX_HEADS), qblk(A_WIDTH), full(2 * IDX_DIM), full(A_WIDTH), full(A_WIDTH),
                  _const_spec(bias_win.shape)],
        out_specs=qblk(A_WIDTH),
        out_shape=jax.ShapeDtypeStruct((b, s, A_WIDTH), BF16),
        scratch_shapes=[pltpu.VMEM((IDX_HEADS * QB, LANE), BF16),
                        pltpu.VMEM((IDX_HEADS, QB, 2 * LANE), F32),
                        pltpu.VMEM((nq + 1, QB, LANE), jnp.int32),
                        pltpu.VMEM((nq + 1, QB, LANE), F32),
                        pltpu.VMEM((nq + 2, QB, LANE), F32)],
        compiler_params=_params(("parallel", "arbitrary")),
        name="attn_a_prompt",
    )(qi, wi, qa, ki2, ka, va, bias_win)


def _mla_prompt_kernel(ql_ref, qr_ref, ckv_ref, kr_ref, o_ref, qst, m_scr, l_scr, acc, *, kb):
    i = pl.program_id(1)
    j = pl.program_id(2)
    nj = pl.num_programs(2)
    last = (i * QB + QB - 1) // kb

    @pl.when(j == 0)
    def _():
        lane = lax.broadcasted_iota(jnp.int32, (QB, 256), 1) // B_ROPE
        qr = qr_ref[0]
        for h in range(B_HEADS):
            qst[h * QB:(h + 1) * QB, 0:256] = ql_ref[0, :, h * 256:(h + 1) * 256]
            qst[h * QB:(h + 1) * QB, 256:512] = jnp.where(lane == h, qr, jnp.zeros_like(qr))
        m_scr[...] = jnp.full(m_scr.shape, NEG_INF, F32)
        l_scr[...] = jnp.zeros(l_scr.shape, F32)
        acc[...] = jnp.zeros(acc.shape, F32)

    @pl.when(j <= last)
    def _():
        ckv = ckv_ref[0]
        s = (_dot_nt(qst[:, 0:256], ckv) + _dot_nt(qst[:, 256:512], kr_ref[0])) * MLA_SCALE
        rows = B_HEADS * QB
        qpos = i * QB + lax.broadcasted_iota(jnp.int32, (rows, kb), 0) % QB
        kpos = j * kb + lax.broadcasted_iota(jnp.int32, (rows, kb), 1)
        s = jnp.where(kpos <= qpos, s, NEG_INF)
        m_old = m_scr[...]
        m_new = jnp.maximum(m_old, jnp.max(s, axis=1, keepdims=True))
        alpha = jnp.exp(m_old - m_new)
        p = jnp.exp(s - m_new)
        l_scr[...] = alpha * l_scr[...] + jnp.sum(p, axis=1, keepdims=True)
        acc[...] = alpha * acc[...] + _dot(p.astype(BF16), ckv)
        m_scr[...] = m_new

    @pl.when(j == nj - 1)
    def _():
        for h in range(B_HEADS):
            sl = slice(h * QB, (h + 1) * QB)
            o_ref[0, :, h * 256:(h + 1) * 256] = (acc[sl, :] / l_scr[sl, :]).astype(BF16)


def _mla_prompt(q_lat, q_rope, ckv, kr_rep):
    b, s, _ = q_lat.shape
    kb = min(MLA_KB, s)
    nq, nk = s // QB, s // kb
    kern = functools.partial(_mla_prompt_kernel, kb=kb)
    kidx = lambda bi, i, j: (bi, jnp.minimum(j, (i * QB + QB - 1) // kb), 0)
    return pl.pallas_call(
        kern,
        grid=(b, nq, nk),
        in_specs=[pl.BlockSpec((1, QB, B_HEADS * B_KV_LORA), lambda bi, i, j: (bi, i, 0)),
                  pl.BlockSpec((1, QB, B_HEADS * B_ROPE), lambda bi, i, j: (bi, i, 0)),
                  pl.BlockSpec((1, kb, B_KV_LORA), kidx),
                  pl.BlockSpec((1, kb, B_HEADS * B_ROPE), kidx)],
        out_specs=pl.BlockSpec((1, QB, B_HEADS * B_KV_LORA), lambda bi, i, j: (bi, i, 0)),
        out_shape=jax.ShapeDtypeStruct((b, s, B_HEADS * B_KV_LORA), BF16),
        scratch_shapes=[pltpu.VMEM((B_HEADS * QB, 512), BF16),
                        pltpu.VMEM((B_HEADS * QB, 1), F32),
                        pltpu.VMEM((B_HEADS * QB, 1), F32),
                        pltpu.VMEM((B_HEADS * QB, B_KV_LORA), F32)],
        compiler_params=_params(("parallel", "parallel", "arbitrary")),
        name="mla_prompt",
    )(q_lat, q_rope, ckv, kr_rep)


def _even_merge_kernel(oa_ref, obl_ref, h_ref, wbuv_ref, wout_ref, o_ref):
    obs = [_dot(obl_ref[:, p * 512:(p + 1) * 512], wbuv_ref[p]).astype(BF16) for p in range(B_HEADS // 2)]
    ob = jnp.concatenate(obs, axis=1)
    o_ref[...] = (h_ref[...] + _dot(oa_ref[...], wout_ref[0:A_WIDTH, :])
                  + _dot(ob, wout_ref[A_WIDTH:A_WIDTH + B_WIDTH, :]))


def _even_merge(oa, obl, h, wts):
    n, d = h.shape
    tm = min(ROW_TILE, n)
    row = lambda w: pl.BlockSpec((tm, w), lambda i: (i, 0))
    return pl.pallas_call(
        _even_merge_kernel,
        grid=(n // tm,),
        in_specs=[row(A_WIDTH), row(B_HEADS * B_KV_LORA), row(d),
                  _const_spec(wts["w_buvbd"].shape), _const_spec(wts["w_out_even"].shape)],
        out_specs=row(d),
        out_shape=jax.ShapeDtypeStruct((n, d), F32),
        compiler_params=_params(("parallel",)),
        name="even_merge",
    )(oa, obl, h, wts["w_buvbd"], wts["w_out_even"])


def _tail_kernel(h_ref, p_ref, gffn_ref, gple_ref, wg_ref, wu_ref, wd_ref, wpg_ref, wpp_ref, gfin_ref, o_ref,
                 *, final, fc):
    h = h_ref[...]
    hn = _rms(h, gffn_ref[...]).astype(BF16)
    dff = wg_ref.shape[1]
    acc = jnp.zeros(h.shape, F32)
    for c in range(dff // fc):
        g = _dot(hn, wg_ref[:, c * fc:(c + 1) * fc])
        u = _dot(hn, wu_ref[:, c * fc:(c + 1) * fc])
        a = (g * jax.nn.sigmoid(g) * u).astype(BF16)
        acc = acc + _dot(a, wd_ref[c * fc:(c + 1) * fc, :])
    h2 = h + acc
    gate = jax.nn.sigmoid(_dot(_rms(h2, gple_ref[...]).astype(BF16), wpg_ref[...]))
    h3 = h2 + gate * _dot(p_ref[...].astype(BF16), wpp_ref[...])
    if final:
        h3 = _rms(h3, gfin_ref[...])
    o_ref[...] = h3


def _tail(h, p, lw, g_final, final):
    n, d = h.shape
    tm = min(ROW_TILE, n)
    row = lambda w: pl.BlockSpec((tm, w), lambda i: (i, 0))
    kern = functools.partial(_tail_kernel, final=final, fc=256)
    return pl.pallas_call(
        kern,
        grid=(n // tm,),
        in_specs=[row(d), row(p.shape[1]), _const_spec((1, d)), _const_spec((1, d)),
                  _const_spec(lw["wg"].shape), _const_spec(lw["wu"].shape), _const_spec(lw["wd"].shape),
                  _const_spec(lw["wpg"].shape), _const_spec(lw["wpp"].shape), _const_spec((1, d))],
        out_specs=row(d),
        out_shape=jax.ShapeDtypeStruct((n, d), F32),
        compiler_params=_params(("parallel",)),
        name="layer_tail",
    )(h, p, lw["g_ffn"], lw["g_ple"], lw["wg"], lw["wu"], lw["wd"], lw["wpg"], lw["wpp"], g_final)


def _odd_proj_kernel(x_ref, g_ref, w_ref, q_o, k_o, kb_o, v_o, vb_o):
    hn = _rms(x_ref[...], g_ref[...]).astype(BF16)
    q_o[...] = _dot(hn, w_ref[:, 0:C_WIDTH]).astype(BF16)
    k = _dot(hn, w_ref[:, C_WIDTH:C_WIDTH + C_KV_WIDTH])
    v = _dot(hn, w_ref[:, C_WIDTH + C_KV_WIDTH:C_WIDTH + 2 * C_KV_WIDTH])
    k_o[...] = k
    kb_o[...] = k.astype(BF16)
    v_o[...] = v
    vb_o[...] = v.astype(BF16)


def _odd_proj(x, g, w):
    n, d = x.shape
    tm = min(ROW_TILE, n)
    row = lambda wd: pl.BlockSpec((tm, wd), lambda i: (i, 0))
    outs = [(C_WIDTH, BF16), (C_KV_WIDTH, F32), (C_KV_WIDTH, BF16), (C_KV_WIDTH, F32), (C_KV_WIDTH, BF16)]
    return pl.pallas_call(
        _odd_proj_kernel,
        grid=(n // tm,),
        in_specs=[row(d), _const_spec((1, d)), _const_spec(w.shape)],
        out_specs=[row(wd) for wd, _ in outs],
        out_shape=[jax.ShapeDtypeStruct((n, wd), dt) for wd, dt in outs],
        compiler_params=_params(("parallel",)),
        name="odd_proj",
    )(x, g, w)


def _swa_prompt_kernel(sink_ref, q_ref, kp_ref, kc_ref, vp_ref, vc_ref, bias_ref, h_ref, wout_ref, o_ref):
    i = pl.program_id(1)
    kk = jnp.concatenate([kp_ref[0], kc_ref[0]], axis=0)
    vv = jnp.concatenate([vp_ref[0], vc_ref[0]], axis=0)
    r = lax.broadcasted_iota(jnp.int32, (QB, 2 * WINDOW), 0)
    jj = lax.broadcasted_iota(jnp.int32, (QB, 2 * WINDOW), 1)
    dist = WINDOW + r - jj
    mask = (dist >= 0) & (dist <= WINDOW) & ((jj >= WINDOW) | (i > 0))
    lane = jj // C_HEAD_DIM
    outs = []
    for g in range(C_GROUP):
        qg = q_ref[0, :, g * 256:(g + 1) * 256]
        og = jnp.zeros((QB, 256), F32)
        for k in range(C_KV_HEADS):
            hidx = k * C_GROUP + g
            qm = jnp.where(lane == k, qg, jnp.zeros_like(qg))
            l = _dot_nt(qm, kk) * C_SCALE + bias_ref[hidx]
            l = jnp.where(mask, l, NEG_INF)
            sk = sink_ref[hidx]
            m = jnp.maximum(jnp.max(l, axis=1, keepdims=True), sk)
            e = jnp.exp(l - m)
            den = jnp.sum(e, axis=1, keepdims=True) + jnp.exp(sk - m)
            og = jnp.where(lane == k, _dot((e / den).astype(BF16), vv), og)
        outs.append(og.astype(BF16))
    o_all = jnp.concatenate(outs, axis=1)
    o_ref[0] = h_ref[0] + _dot(o_all, wout_ref[...])


def _swa_prompt(q, k, v, bias_raw, sinks, h, w_out):
    b, s, d = h.shape
    nq = s // QB
    cur = lambda w: pl.BlockSpec((1, QB, w), lambda bi, i: (bi, i, 0))
    prev = lambda w: pl.BlockSpec((1, QB, w), lambda bi, i: (bi, jnp.maximum(i - 1, 0), 0))
    return pl.pallas_call(
        _swa_prompt_kernel,
        grid=(b, nq),
        in_specs=[pl.BlockSpec(memory_space=pltpu.SMEM),
                  cur(C_WIDTH), prev(C_KV_WIDTH), cur(C_KV_WIDTH), prev(C_KV_WIDTH), cur(C_KV_WIDTH),
                  _const_spec(bias_raw.shape), cur(d), _const_spec(w_out.shape)],
        out_specs=cur(d),
        out_shape=jax.ShapeDtypeStruct((b, s, d), F32),
        compiler_params=_params(("parallel", "parallel")),
        name="swa_prompt",
    )(sinks, q, k, k, v, v, bias_raw, h, w_out)


def _rel_bucket(dist):
    n = jnp.maximum(dist, 0)
    max_exact = REL_BUCKETS // 2
    nf = jnp.maximum(n, 1).astype(F32)
    large = max_exact + (jnp.log(nf / max_exact) / math.log(REL_MAX_DIST / max_exact)
                         * (REL_BUCKETS - max_exact)).astype(jnp.int32)
    large = jnp.minimum(large, REL_BUCKETS - 1)
    return jnp.where(n < max_exact, n, large)


def _bias_window(rel_table):
    r = jnp.arange(QB)[:, None]
    j = jnp.arange(2 * QB)[None, :]
    return jnp.transpose(rel_table[_rel_bucket(WINDOW + r - j)].astype(F32), (2, 0, 1))


def _rope_tables(pos, reps):
    half = B_ROPE // 2
    inv = ROPE_THETA ** (-jnp.arange(half, dtype=F32) / half)
    ang = pos.astype(F32)[:, None] * inv
    cos, sin = jnp.cos(ang), jnp.sin(ang)
    cos_f = jnp.tile(jnp.concatenate([cos, cos], axis=1), (1, reps))
    sin_f = jnp.tile(jnp.concatenate([-sin, sin], axis=1), (1, reps))
    return cos_f, sin_f


def _block_diag_pairs(m):
    z = jnp.zeros_like(m[0])
    return jnp.stack([jnp.concatenate([jnp.concatenate([m[2 * p], z], axis=1),
                                       jnp.concatenate([z, m[2 * p + 1]], axis=1)], axis=0)
                      for p in range(m.shape[0] // 2)])


def _prep_even_weights(w_in, w_out, g_bq, w_buq, g_bkv, w_buk, w_buv):
    d = w_in.shape[0]
    splits = np.cumsum([A_WIDTH, A_WIDTH, A_WIDTH, IDX_WIDTH, IDX_DIM, IDX_HEADS, B_Q_LORA, B_KV_LORA])
    qa, ka, va, qi, ki, wi, cq, ckv, kr = jnp.split(w_in, splits.tolist(), axis=1)
    wi_pad = jnp.concatenate([wi, jnp.zeros((d, LANE - IDX_HEADS), w_in.dtype)], axis=1)
    w_all = jnp.concatenate([qa, ka, va, qi, cq, ckv, ki, ki, jnp.tile(kr, (1, B_HEADS)), wi_pad], axis=1)
    wq = w_buq.reshape(B_Q_LORA, B_HEADS, B_NOPE + B_ROPE)
    w_qn = wq[:, :, :B_NOPE].reshape(B_Q_LORA, B_HEADS * B_NOPE)
    w_qr = wq[:, :, B_NOPE:].reshape(B_Q_LORA, B_HEADS * B_ROPE)
    w_bukt = jnp.transpose(w_buk, (1, 2, 0))
    w_buvh = jnp.transpose(w_buv, (1, 0, 2))
    return {
        "w_in": w_all.astype(BF16),
        "g_bq": g_bq.reshape(1, -1), "g_bkv": g_bkv.reshape(1, -1),
        "w_qn": w_qn.astype(BF16), "w_qr": w_qr.astype(BF16),
        "w_bukbd": _block_diag_pairs(w_bukt).astype(BF16),
        "w_buvbd": _block_diag_pairs(w_buvh).astype(BF16),
        "w_out_even": w_out.astype(BF16),
    }


def _prep_odd_weights(w_in, w_out):
    d = w_in.shape[0]
    q = w_in[:, :C_WIDTH].reshape(d, C_KV_HEADS, C_GROUP, C_HEAD_DIM)
    q = jnp.transpose(q, (0, 2, 1, 3)).reshape(d, C_WIDTH)
    w_in_p = jnp.concatenate([q, w_in[:, C_WIDTH:]], axis=1).astype(BF16)
    wo = w_out.reshape(C_KV_HEADS, C_GROUP, C_HEAD_DIM, -1)
    wo = jnp.transpose(wo, (1, 0, 2, 3)).reshape(C_WIDTH, -1).astype(BF16)
    return w_in_p, wo


def _sample_even_attention(pe, bd, t, cache_a_k, cache_a_v, cache_a_idx, cache_b_ckv, cache_b_krope,
                           page_table, e, rel_table):
    past = page_table.shape[1] * PAGE
    pos = past + jnp.arange(t, dtype=jnp.int32)
    n_keys = past + t
    f = lambda a, *s: a.astype(F32).reshape(bd, t, *s)
    qa, ka, va = f(pe["qa"], A_HEADS, A_HEAD_DIM), f(pe["ka"], A_HEADS, A_HEAD_DIM), f(pe["va"], A_HEADS, A_HEAD_DIM)
    qi, ki, wi = f(pe["qi"], IDX_HEADS, IDX_DIM), f(pe["ki"], IDX_DIM), f(pe["wi"], IDX_HEADS)
    q_lat, q_rope = f(pe["q_lat"], B_HEADS, B_KV_LORA), f(pe["q_rope"], B_HEADS, B_ROPE)
    ckv, kr = f(pe["ckv"], B_KV_LORA), f(pe["kr"], B_ROPE)
    layer_pages = jnp.full(page_table.shape, e, jnp.int32)

    def past_rows(pool):
        g = pool[layer_pages, page_table]
        return g.reshape(bd, past, *g.shape[3:])

    ki_all = jnp.concatenate([past_rows(cache_a_idx), ki], axis=1)
    n_sel = min(TOPK_MAX, n_keys // 4)
    dots = jnp.einsum('bthd,bsd->bths', qi, ki_all).astype(F32)
    scores = jnp.einsum('bth,bths->bts', wi, jax.nn.relu(dots))
    k_pos = jnp.arange(n_keys, dtype=jnp.int32)
    scores = jnp.where((k_pos[None, :] <= pos[:, None])[None], scores, -jnp.inf)
    _, sel = lax.top_k(scores, n_sel)
    valid = sel <= pos[None, :, None]
    in_past = sel < past
    sp = jnp.minimum(sel, past - 1)
    phys = jax.vmap(lambda pt, lp: pt[lp])(page_table, sp // PAGE)
    off = sp % PAGE
    layer_sel = jnp.full(sel.shape, e, jnp.int32)
    sn = jnp.clip(sel - past, 0, t - 1)
    take_rows = jax.vmap(lambda rows, idx: rows[idx])

    def gather_sel(pool, new):
        return jnp.where(in_past[..., None, None], pool[layer_sel, phys, off], take_rows(new, sn))

    k_sel, v_sel = gather_sel(cache_a_k, ka), gather_sel(cache_a_v, va)
    logits = jnp.einsum('bthd,btkhd->bthk', qa, k_sel).astype(F32) * A_SCALE
    bias = rel_table[_rel_bucket(pos[None, :, None] - sel)].astype(F32)
    logits = logits + jnp.swapaxes(bias, -1, -2)
    logits = jnp.where(valid[:, :, None, :], logits, -jnp.inf)
    probs = jax.nn.softmax(logits, axis=-1)
    oa = jnp.einsum('bthk,btkhd->bthd', probs, v_sel)

    t_idx = jnp.arange(t)
    segs = [(past_rows(cache_b_ckv), past_rows(cache_b_krope), jnp.ones((t, past), dtype=bool)),
            (ckv, kr, t_idx[None, :] <= t_idx[:, None])]
    lgs = []
    for c, r, mask in segs:
        lg = (jnp.einsum('bthc,blc->bhtl', q_lat, c) + jnp.einsum('bthr,blr->bhtl', q_rope, r)).astype(F32) * MLA_SCALE
        lgs.append(jnp.where(mask, lg, -jnp.inf))
    probs = jax.nn.softmax(jnp.concatenate(lgs, axis=-1), axis=-1)
    ob = (jnp.einsum('bhtl,blc->bthc', probs[..., :past], segs[0][0])
          + jnp.einsum('bhtl,blc->bthc', probs[..., past:], ckv))
    return (oa.reshape(bd * t, A_WIDTH).astype(BF16), ob.reshape(bd * t, B_HEADS * B_KV_LORA).astype(BF16))


def _sample_odd_attention(q, k, v, buf_k, buf_v, sinks, rel_table, bd, t):
    past_end = 0
    q = q.astype(F32).reshape(bd, t, C_GROUP, C_KV_HEADS, C_HEAD_DIM)
    k = k.reshape(bd, t, C_KV_HEADS, C_HEAD_DIM)
    v = v.reshape(bd, t, C_KV_HEADS, C_HEAD_DIM)
    wb = buf_k.shape[1]
    kk = jnp.concatenate([buf_k, k], axis=1)
    vv = jnp.concatenate([buf_v, v], axis=1)
    key_pos = past_end - wb + jnp.arange(wb + t, dtype=jnp.int32)
    q_pos = past_end + jnp.arange(t, dtype=jnp.int32)
    dist = q_pos[:, None] - key_pos[None, :]
    mask = (dist >= 0) & (dist <= WINDOW)
    bias = rel_table[_rel_bucket(dist)].astype(F32)
    bias = bias.reshape(t, wb + t, C_KV_HEADS, C_GROUP).transpose(3, 2, 0, 1)
    logits = jnp.einsum('btgkd,bskd->bgkts', q, kk).astype(F32) * C_SCALE + bias
    logits = jnp.where(mask, logits, -jnp.inf)
    sk = sinks.reshape(C_KV_HEADS, C_GROUP).T.astype(F32)[None, :, :, None, None]
    m = jnp.maximum(jnp.max(logits, axis=-1, keepdims=True), sk)
    ex = jnp.exp(logits - m)
    probs = ex / (jnp.sum(ex, axis=-1, keepdims=True) + jnp.exp(sk - m))
    out = jnp.einsum('bgkts,bskd->btgkd', probs, vv).reshape(bd * t, C_WIDTH)
    return out.astype(BF16), kk[:, t:], vv[:, t:]


def _out_proj_kernel(o_ref, h_ref, w_ref, out_ref):
    out_ref[...] = h_ref[...] + _dot(o_ref[...], w_ref[...])


def _out_proj(o, h, w):
    n, d = h.shape
    tm = min(ROW_TILE, n)
    row = lambda wd: pl.BlockSpec((tm, wd), lambda i: (i, 0))
    return pl.pallas_call(
        _out_proj_kernel,
        grid=(n // tm,),
        in_specs=[row(o.shape[1]), row(d), _const_spec(w.shape)],
        out_specs=row(d),
        out_shape=jax.ShapeDtypeStruct((n, d), F32),
        compiler_params=_params(("parallel",)),
        name="out_proj",
    )(o, h, w)


def kernel(x_prompt, x_sample, cache_a_k, cache_a_v, cache_a_idx, cache_b_ckv, cache_b_krope, state_c_k, state_c_v, page_table, p_prompt, p_sample, rel_table, w_in_even, w_out_even, g_bq, w_buq, g_bkv, w_buk, w_buv, w_in_odd, w_out_odd, c_sinks, g_mix, g_ffn, w_ffn_gate, w_ffn_up, w_ffn_down, g_ple, w_ple_gate, w_ple_proj, g_final):
    b, s, d = x_prompt.shape
    bd, t, _ = x_sample.shape
    depth = g_mix.shape[0]
    assert depth == 2 and w_in_even.shape[0] == 1 and w_in_odd.shape[0] == 1
    assert s % 256 == 0 and rel_table.shape == (REL_BUCKETS, A_HEADS)
    past = page_table.shape[1] * PAGE

    we = _prep_even_weights(w_in_even[0], w_out_even[0], g_bq[0], w_buq[0], g_bkv[0], w_buk[0], w_buv[0])
    w_in_o, w_out_o = _prep_odd_weights(w_in_odd[0], w_out_odd[0])
    lws = [{"g_ffn": g_ffn[i].reshape(1, d), "g_ple": g_ple[i].reshape(1, d),
            "wg": w_ffn_gate[i].astype(BF16), "wu": w_ffn_up[i].astype(BF16), "wd": w_ffn_down[i].astype(BF16),
            "wpg": w_ple_gate[i].astype(BF16), "wpp": w_ple_proj[i].astype(BF16)} for i in range(depth)]
    gfin = g_final.reshape(1, d)
    bias_raw = _bias_window(rel_table)
    bias_rel = bias_raw - rel_table[REL_BUCKETS - 1].astype(F32)[:, None, None]
    cos_p, sin_p = _rope_tables(jnp.arange(s, dtype=jnp.int32), B_HEADS)
    cos_s, sin_s = _rope_tables(jnp.tile(past + jnp.arange(t, dtype=jnp.int32), bd), B_HEADS)

    hp = x_prompt.reshape(b * s, d)
    hs = x_sample.reshape(bd * t, d)
    g0 = g_mix[0].reshape(1, d)
    g1 = g_mix[1].reshape(1, d)

    pe = _even_proj(hp, g0, we, cos_p, sin_p)
    r3 = lambda a: a.reshape(b, s, a.shape[-1])
    oa = _attn_a_prompt(r3(pe["qi"]), r3(pe["wi"]), r3(pe["qa"]), r3(pe["ki2"]), r3(pe["ka_bf"]), r3(pe["va_bf"]),
                        bias_rel)
    obl = _mla_prompt(r3(pe["q_lat"]), r3(pe["q_rope"]), r3(pe["ckv_bf"]), r3(pe["kr_rep"]))
    hp = _even_merge(oa.reshape(b * s, A_WIDTH), obl.reshape(b * s, -1), hp, we)
    hp = _tail(hp, p_prompt[0].reshape(b * s, -1), lws[0], gfin, False)

    se = _even_proj(hs, g0, we, cos_s, sin_s)
    oa_s, obl_s = _sample_even_attention(se, bd, t, cache_a_k, cache_a_v, cache_a_idx, cache_b_ckv, cache_b_krope,
                                         page_table, 0, rel_table)
    hs = _even_merge(oa_s, obl_s, hs, we)
    hs = _tail(hs, p_sample[0].reshape(bd * t, -1), lws[0], gfin, False)

    q, k, kb, v, vb = _odd_proj(hp, g1, w_in_o)
    hp = _swa_prompt(q.reshape(b, s, -1), kb.reshape(b, s, -1), vb.reshape(b, s, -1), bias_raw, c_sinks[0],
                     hp.reshape(b, s, d), w_out_o).reshape(b * s, d)
    y_prompt = _tail(hp, p_prompt[1].reshape(b * s, -1), lws[1], gfin, True)
    wp = min(WINDOW, s)
    pc_k = k.reshape(b, s, C_KV_HEADS, C_HEAD_DIM)[:, s - wp:]
    pc_v = v.reshape(b, s, C_KV_HEADS, C_HEAD_DIM)[:, s - wp:]

    qs, ks, _, vs, _ = _odd_proj(hs, g1, w_in_o)
    os_, sc_k, sc_v = _sample_odd_attention(qs, ks, vs, state_c_k[0], state_c_v[0], c_sinks[0], rel_table, bd, t)
    hs = _out_proj(os_, hs, w_out_o)
    y_sample = _tail(hs, p_sample[1].reshape(bd * t, -1), lws[1], gfin, True)

    hd = (A_HEADS, A_HEAD_DIM)
    return (y_prompt.reshape(b, s, d), y_sample.reshape(bd, t, d),
            pe["ka"].reshape(1, b, s, *hd), pe["va"].reshape(1, b, s, *hd), pe["ki"].reshape(1, b, s, IDX_DIM),
            pe["ckv"].reshape(1, b, s, B_KV_LORA), pe["kr"].reshape(1, b, s, B_ROPE),
            pc_k[None], pc_v[None],
            se["ka"].reshape(1, bd, t, *hd), se["va"].reshape(1, bd, t, *hd), se["ki"].reshape(1, bd, t, IDX_DIM),
            se["ckv"].reshape(1, bd, t, B_KV_LORA), se["kr"].reshape(1, bd, t, B_ROPE),
            sc_k[None], sc_v[None])
```

```python
import functools
import math

import numpy as np
import jax
import jax.numpy as jnp
from jax import lax
from jax.experimental import pallas as pl
from jax.experimental.pallas import tpu as pltpu

F32 = jnp.float32
BF16 = jnp.bfloat16

RMS_EPS = 1e-6
A_HEADS, A_HEAD_DIM = 16, 32
A_WIDTH = A_HEADS * A_HEAD_DIM
IDX_HEADS, IDX_DIM = 8, 64
IDX_WIDTH = IDX_HEADS * IDX_DIM
IDX_SCALE = IDX_WIDTH ** -0.5
TOPK_MAX = 256
B_HEADS, B_NOPE, B_ROPE, B_V = 8, 64, 32, 64
B_Q_LORA, B_KV_LORA = 256, 256
B_WIDTH = B_HEADS * B_V
MLA_SCALE = (B_NOPE + B_ROPE) ** -0.5
ROPE_THETA = 10000.0
C_HEADS, C_KV_HEADS, C_HEAD_DIM = 16, 4, 64
C_GROUP = C_HEADS // C_KV_HEADS
C_WIDTH = C_HEADS * C_HEAD_DIM
C_KV_WIDTH = C_KV_HEADS * C_HEAD_DIM
WINDOW = 128
C_SCALE = C_HEAD_DIM ** -0.5
A_SCALE = A_HEAD_DIM ** -0.5
REL_BUCKETS, REL_MAX_DIST = 32, 128
PAGE = 128
QB = 128
LANE = 128
MLA_KB = 512
ROW_TILE = 256
VMEM_LIMIT = 56 * 1024 * 1024

INT_MIN = np.int32(-2 ** 31)
KEY_NEG = np.int32(-2139095041)
NEG_INF = float("-inf")

_NT = (((1,), (1,)), ((), ()))


def _dot(a, b):
    return jnp.dot(a, b, preferred_element_type=F32)


def _dot_nt(a, b):
    return lax.dot_general(a, b, _NT, preferred_element_type=F32)


def _rms(x, g):
    ms = jnp.mean(x * x, axis=-1, keepdims=True)
    return x * lax.rsqrt(ms + RMS_EPS) * g


def _const_spec(shape):
    nd = len(shape)
    return pl.BlockSpec(shape, lambda *_: (0,) * nd, pipeline_mode=pl.Buffered(1))


def _params(sem):
    return pltpu.CompilerParams(dimension_semantics=sem, vmem_limit_bytes=VMEM_LIMIT)


def _rope_apply(x, cos, sin_signed):
    n = x.shape[1]
    lane = lax.broadcasted_iota(jnp.int32, x.shape, 1)
    partner = jnp.where((lane % B_ROPE) < (B_ROPE // 2),
                        pltpu.roll(x, n - B_ROPE // 2, 1), pltpu.roll(x, B_ROPE // 2, 1))
    return x * cos + partner * sin_signed


def _even_proj_kernel(x_ref, g_ref, w_ref, gq_ref, wn_ref, wr_ref, wk_ref, gkv_ref, cos_ref, sin_ref,
                      qa_o, ka_o, kab_o, va_o, vab_o, qi_o, ki_o, ki2_o, wi_o, ql_o, qr_o,
                      ckv_o, ckvb_o, kr_o, krr_o):
    hn = _rms(x_ref[...], g_ref[...]).astype(BF16)
    aw = A_WIDTH
    qa_o[...] = _dot(hn, w_ref[:, 0:aw]).astype(BF16)
    ka = _dot(hn, w_ref[:, aw:2 * aw])
    ka_o[...] = ka
    kab_o[...] = ka.astype(BF16)
    va = _dot(hn, w_ref[:, 2 * aw:3 * aw])
    va_o[...] = va
    vab_o[...] = va.astype(BF16)
    qi_o[...] = _dot(hn, w_ref[:, 3 * aw:4 * aw]).astype(BF16)
    c0 = 4 * aw
    cq = _dot(hn, w_ref[:, c0:c0 + B_Q_LORA])
    ckv = _dot(hn, w_ref[:, c0 + B_Q_LORA:c0 + B_Q_LORA + B_KV_LORA])
    c1 = c0 + B_Q_LORA + B_KV_LORA
    ki2 = _dot(hn, w_ref[:, c1:c1 + 2 * IDX_DIM])
    krr = _dot(hn, w_ref[:, c1 + LANE:c1 + LANE + 256])
    wi = _dot(hn, w_ref[:, c1 + LANE + 256:c1 + 2 * LANE + 256])
    ki_o[...] = ki2[:, 0:IDX_DIM]
    ki2_o[...] = ki2.astype(BF16)
    wi_o[...] = wi[:, 0:IDX_HEADS] * IDX_SCALE
    cos = cos_ref[...]
    sin = sin_ref[...]
    krr = _rope_apply(krr, cos, sin)
    kr_o[...] = krr[:, 0:B_ROPE]
    krr_o[...] = krr.astype(BF16)
    ckvn = _rms(ckv, gkv_ref[...])
    ckv_o[...] = ckvn
    ckvb_o[...] = ckvn.astype(BF16)
    cqn = _rms(cq, gq_ref[...]).astype(BF16)
    qn = _dot(cqn, wn_ref[...]).astype(BF16)
    qr = _dot(cqn, wr_ref[...])
    qr_o[...] = _rope_apply(qr, cos, sin).astype(BF16)
    for p in range(B_HEADS // 2):
        ql_o[:, p * 512:(p + 1) * 512] = _dot(qn[:, p * LANE:(p + 1) * LANE], wk_ref[p]).astype(BF16)


def _even_proj(x, g, wts, cos, sin):
    n, d = x.shape
    tm = min(ROW_TILE, n)
    nblk = n // tm
    tblk = cos.shape[0] // tm
    row = lambda w: pl.BlockSpec((tm, w), lambda i: (i, 0))
    tab = pl.BlockSpec((tm, 256), lambda i: (i % tblk, 0))
    outs = [(A_WIDTH, BF16), (A_WIDTH, F32), (A_WIDTH, BF16), (A_WIDTH, F32), (A_WIDTH, BF16),
            (IDX_WIDTH, BF16), (IDX_DIM, F32), (2 * IDX_DIM, BF16), (IDX_HEADS, F32),
            (B_HEADS * B_KV_LORA, BF16), (B_HEADS * B_ROPE, BF16),
            (B_KV_LORA, F32), (B_KV_LORA, BF16), (B_ROPE, F32), (B_HEADS * B_ROPE, BF16)]
    res = pl.pallas_call(
        _even_proj_kernel,
        grid=(nblk,),
        in_specs=[row(d), _const_spec((1, d)), _const_spec(wts["w_in"].shape),
                  _const_spec((1, B_Q_LORA)), _const_spec(wts["w_qn"].shape), _const_spec(wts["w_qr"].shape),
                  _const_spec(wts["w_bukbd"].shape), _const_spec((1, B_KV_LORA)), tab, tab],
        out_specs=[row(w) for w, _ in outs],
        out_shape=[jax.ShapeDtypeStruct((n, w), dt) for w, dt in outs],
        compiler_params=_params(("parallel",)),
        name="even_proj",
    )(x, g, wts["w_in"], wts["g_bq"], wts["w_qn"], wts["w_qr"], wts["w_bukbd"], wts["g_bkv"], cos, sin)
    names = ["qa", "ka", "ka_bf", "va", "va_bf", "qi", "ki", "ki2", "wi", "q_lat", "q_rope",
             "ckv", "ckv_bf", "kr", "kr_rep"]
    return dict(zip(names, res))


def _sortable_key(x):
    x = jnp.where(x == 0.0, 0.0, x)
    bits = pltpu.bitcast(x, jnp.int32)
    return jnp.where(bits < 0, bits ^ jnp.int32(0x7FFFFFFF), bits)


def _topk_mask(keybuf, maskb, cbuf, npairs, n_sel, idx_bits):
    rows = keybuf.shape[1]
    lane1 = lax.broadcasted_iota(jnp.int32, (rows, LANE), 1)

    def count(pred):
        def body(c, cnt):
            a = jnp.where(pred(keybuf[2 * c], 2 * c), 1.0, 0.0)
            b = jnp.where(pred(keybuf[2 * c + 1], 2 * c + 1), 1.0, 0.0)
            return cnt + (a + b)
        cnt = lax.fori_loop(0, npairs, body, jnp.zeros((rows, LANE), F32))
        return jnp.sum(cnt, axis=1, keepdims=True)

    def bit_step(b, t_u):
        cand_u = t_u | (jnp.int32(1) << (31 - b))
        cbuf[...] = jnp.broadcast_to(cand_u ^ INT_MIN, (rows, LANE))
        tot = count(lambda kk, c: kk >= cbuf[...])
        return jnp.where(tot >= n_sel, cand_u, t_u)

    t_u = lax.fori_loop(0, 32, bit_step, jnp.zeros((rows, 1), jnp.int32))
    t_s = t_u ^ INT_MIN
    tb = jnp.broadcast_to(t_s, (rows, LANE))
    cbuf[...] = tb
    cge = count(lambda kk, c: kk >= cbuf[...])
    tie_rows = jnp.logical_and(t_s > KEY_NEG, cge > n_sel)
    any_tie = jnp.max(jnp.where(tie_rows, 1.0, 0.0)) > 0.0

    @pl.when(jnp.logical_not(any_tie))
    def _():
        def body(c, carry):
            kk = keybuf[c]
            maskb[c] = jnp.where(jnp.logical_and(kk >= cbuf[...], kk > KEY_NEG), 0.0, NEG_INF)
            return carry
        lax.fori_loop(0, 2 * npairs, body, 0)

    @pl.when(any_tie)
    def _():
        need = n_sel - count(lambda kk, c: kk > tb)

        def jstep(b, j_u):
            cand = j_u | (jnp.int32(1) << (idx_bits - 1 - b))
            cb = jnp.broadcast_to(cand, (rows, LANE))
            f = count(lambda kk, c: jnp.logical_and(kk == tb, (c * LANE + lane1) < cb))
            return jnp.where(f < need, cand, j_u)

        j_u = lax.fori_loop(0, idx_bits, jstep, jnp.zeros((rows, 1), jnp.int32))
        jb = jnp.broadcast_to(j_u, (rows, LANE))

        def body(c, carry):
            kk = keybuf[c]
            keep = jnp.logical_or(kk > tb, jnp.logical_and(kk == tb, (c * LANE + lane1) <= jb))
            maskb[c] = jnp.where(jnp.logical_and(keep, kk > KEY_NEG), 0.0, NEG_INF)
            return carry
        lax.fori_loop(0, 2 * npairs, body, 0)


def _attn_a_prompt_kernel(qi_ref, wi_ref, qa_ref, ki2_ref, ka_ref, va_ref, bias_ref, o_ref,
                          qst, wb, keybuf, maskb, cbuf, qstk, mrun, ssum, acc, pbuf, *, n_sel, idx_bits):
    i = pl.program_id(1)
    nk2 = (i + 2) // 2
    lane1 = lax.broadcasted_iota(jnp.int32, (QB, LANE), 1)
    for h in range(IDX_HEADS):
        blk = qi_ref[0, :, (h // 2) * LANE:(h // 2 + 1) * LANE]
        qst[h * QB:(h + 1) * QB, :] = jnp.where((lane1 // IDX_DIM) == (h % 2), blk, jnp.zeros_like(blk))
        wb[h] = jnp.broadcast_to(wi_ref[0, :, h:h + 1], (QB, 2 * LANE))
    row2 = lax.broadcasted_iota(jnp.int32, (QB, 2 * LANE), 0)
    col2 = lax.broadcasted_iota(jnp.int32, (QB, 2 * LANE), 1)

    def score_chunk(c, carry):
        kc = ki2_ref[0, pl.ds(pl.multiple_of(c * 256, 256), 256), :]
        d = _dot_nt(qst[...], kc)
        acc_s = wb[0] * jnp.maximum(d[0:QB, :], 0.0)
        for h in range(1, IDX_HEADS):
            acc_s = acc_s + wb[h] * jnp.maximum(d[h * QB:(h + 1) * QB, :], 0.0)
        key = _sortable_key(acc_s)
        valid = (c * 256 + col2) <= (i * QB + row2)
        key = jnp.where(valid, key, KEY_NEG)
        keybuf[2 * c] = key[:, 0:LANE]
        keybuf[2 * c + 1] = key[:, LANE:2 * LANE]
        return carry

    lax.fori_loop(0, nk2, score_chunk, 0)
    _topk_mask(keybuf, maskb, cbuf, nk2, n_sel, idx_bits)

    lane2 = col2 // A_HEAD_DIM
    nfar = jnp.maximum(nk2 - 2, 0)
    ngrp = A_HEADS // 8
    for g in range(ngrp):
        qg = qa_ref[0, :, g * 256:(g + 1) * 256]
        for hh in range(8):
            qstk[g, hh * QB:(hh + 1) * QB, :] = jnp.where(lane2 == hh, qg, jnp.zeros_like(qg))
    mrun[...] = jnp.full(mrun.shape, NEG_INF, F32)
    ssum[...] = jnp.zeros(ssum.shape, F32)
    acc[...] = jnp.zeros(acc.shape, F32)

    def logits(c, g, hh, s_all, near):
        s = s_all[hh * QB:(hh + 1) * QB, :] * A_SCALE
        halves = []
        for hf in range(2):
            l = s[:, hf * LANE:(hf + 1) * LANE] + maskb[2 * c + hf]
            if near:
                blk = 2 * c + hf
                bh = bias_ref[g * 8 + hh]
                l = l + jnp.where(blk == i, bh[:, LANE:2 * LANE],
                                  jnp.where(blk == i - 1, bh[:, 0:LANE], 0.0))
            halves.append(l)
        return halves

    def pass1(near):
        def body(c, carry):
            rows = pl.ds(pl.multiple_of(c * 256, 256), 256)
            for g in range(ngrp):
                s_all = _dot_nt(qstk[g], ka_ref[0, rows, g * 256:(g + 1) * 256])
                for hh in range(8):
                    l0, l1 = logits(c, g, hh, s_all, near)
                    sl = slice(hh * QB, (hh + 1) * QB)
                    mrun[g, sl, :] = jnp.maximum(mrun[g, sl, :], jnp.maximum(l0, l1))
            return carry
        return body

    lax.fori_loop(0, nfar, pass1(False), 0)
    lax.fori_loop(nfar, nk2, pass1(True), 0)
    for g in range(ngrp):
        mrun[g] = jnp.broadcast_to(jnp.max(mrun[g], axis=1, keepdims=True), mrun.shape[1:])

    def pass2(near):
        def body(c, carry):
            rows = pl.ds(pl.multiple_of(c * 256, 256), 256)
            for g in range(ngrp):
                s_all = _dot_nt(qstk[g], ka_ref[0, rows, g * 256:(g + 1) * 256])
                for hh in range(8):
                    l0, l1 = logits(c, g, hh, s_all, near)
                    sl = slice(hh * QB, (hh + 1) * QB)
                    m = mrun[g, sl, :]
                    p0 = jnp.exp(l0 - m)
                    p1 = jnp.exp(l1 - m)
                    ssum[g, sl, :] = ssum[g, sl, :] + (p0 + p1)
                    pbuf[g, sl, 0:LANE] = p0.astype(BF16)
                    pbuf[g, sl, LANE:2 * LANE] = p1.astype(BF16)
                acc[g] = acc[g] + _dot(pbuf[g], va_ref[0, rows, g * 256:(g + 1) * 256])
            return carry
        return body

    lax.fori_loop(0, nfar, pass2(False), 0)
    lax.fori_loop(nfar, nk2, pass2(True), 0)
    for g in range(ngrp):
        out_g = jnp.zeros((QB, 256), F32)
        for hh in range(8):
            sl = slice(hh * QB, (hh + 1) * QB)
            den = jnp.sum(ssum[g, sl, :], axis=1, keepdims=True)
            out_g = jnp.where(lane2 == hh, acc[g, sl, :] / den, out_g)
        o_ref[0, :, g * 256:(g + 1) * 256] = out_g.astype(BF16)


def _attn_a_prompt(qi, wi, qa, ki2, ka, va, bias_win):
    b, s, _ = qa.shape
    nq = s // QB
    n_sel = min(TOPK_MAX, s // 4)
    idx_bits = int(math.log2(s)) + 1
    kern = functools.partial(_attn_a_prompt_kernel, n_sel=n_sel, idx_bits=idx_bits)
    qblk = lambda w: pl.BlockSpec((1, QB, w), lambda bi, i: (bi, i, 0))
    full = lambda w: pl.BlockSpec((1, s, w), lambda bi, i: (bi, 0, 0))
    return pl.pallas_call(
        kern,
        grid=(b, nq),
        in_specs=[qblk(IDX_WIDTH), qblk(IDX_HEADS), qblk(A_WIDTH), full(2 * IDX_DIM), full(A_WIDTH), full(A_WIDTH),
                  _const_spec(bias_win.shape)],
        out_specs=qblk(A_WIDTH),
        out_shape=jax.ShapeDtypeStruct((b, s, A_WIDTH), BF16),
        scratch_shapes=[pltpu.VMEM((IDX_HEADS * QB, LANE), BF16),
                        pltpu.VMEM((IDX_HEADS, QB, 2 * LANE), F32),
                        pltpu.VMEM((nq, QB, LANE), jnp.int32),
                        pltpu.VMEM((nq, QB, LANE), F32),
                        pltpu.VMEM((QB, LANE), jnp.int32),
                        pltpu.VMEM((A_HEADS // 8, 8 * QB, 256), BF16),
                        pltpu.VMEM((A_HEADS // 8, 8 * QB, LANE), F32),
                        pltpu.VMEM((A_HEADS // 8, 8 * QB, LANE), F32),
                        pltpu.VMEM((A_HEADS // 8, 8 * QB, 256), F32),
                        pltpu.VMEM((A_HEADS // 8, 8 * QB, 256), BF16)],
        compiler_params=_params(("parallel", "arbitrary")),
        name="attn_a_prompt",
    )(qi, wi, qa, ki2, ka, va, bias_win)


def _mla_prompt_kernel(ql_ref, qr_ref, ckv_ref, kr_ref, o_ref, qst, m_scr, l_scr, acc, pbuf, *, kb):
    i = pl.program_id(1)
    j = pl.program_id(2)
    nj = pl.num_programs(2)
    last = (i * QB + QB - 1) // kb

    @pl.when(j == 0)
    def _():
        lane = lax.broadcasted_iota(jnp.int32, (QB, 256), 1) // B_ROPE
        qr = qr_ref[0]
        for h in range(B_HEADS):
            qst[h * QB:(h + 1) * QB, 0:256] = ql_ref[0, :, h * 256:(h + 1) * 256]
            qst[h * QB:(h + 1) * QB, 256:512] = jnp.where(lane == h, qr, jnp.zeros_like(qr))
        m_scr[...] = jnp.full(m_scr.shape, NEG_INF, F32)
        l_scr[...] = jnp.zeros(l_scr.shape, F32)
        acc[...] = jnp.zeros(acc.shape, F32)

    def step(diag):
        ckv = ckv_ref[0]
        s_all = _dot_nt(qst[:, 0:256], ckv) + _dot_nt(qst[:, 256:512], kr_ref[0])
        if diag:
            qpos = i * QB + lax.broadcasted_iota(jnp.int32, (QB, kb), 0)
            kpos = j * kb + lax.broadcasted_iota(jnp.int32, (QB, kb), 1)
            ok = kpos <= qpos
        for h in range(B_HEADS):
            sl = slice(h * QB, (h + 1) * QB)
            s = s_all[sl, :] * MLA_SCALE
            if diag:
                s = jnp.where(ok, s, NEG_INF)
            m_old = m_scr[sl, :]
            m_new = jnp.maximum(m_old, jnp.max(s, axis=1, keepdims=True))
            alpha = jnp.exp(m_old - m_new)
            p = jnp.exp(s - m_new)
            l_scr[sl, :] = alpha * l_scr[sl, :] + jnp.sum(p, axis=1, keepdims=True)
            acc[sl, :] = alpha * acc[sl, :]
            m_scr[sl, :] = m_new
            pbuf[sl, :] = p.astype(BF16)
        acc[...] = acc[...] + _dot(pbuf[...], ckv)

    @pl.when(j < last)
    def _():
        step(False)

    @pl.when(j == last)
    def _():
        step(True)

    @pl.when(j == nj - 1)
    def _():
        for h in range(B_HEADS):
            sl = slice(h * QB, (h + 1) * QB)
            o_ref[0, :, h * 256:(h + 1) * 256] = (acc[sl, :] / l_scr[sl, :]).astype(BF16)


def _mla_prompt(q_lat, q_rope, ckv, kr_rep):
    b, s, _ = q_lat.shape
    kb = min(MLA_KB, s)
    nq, nk = s // QB, s // kb
    kern = functools.partial(_mla_prompt_kernel, kb=kb)
    kidx = lambda bi, i, j: (bi, jnp.minimum(j, (i * QB + QB - 1) // kb), 0)
    return pl.pallas_call(
        kern,
        grid=(b, nq, nk),
        in_specs=[pl.BlockSpec((1, QB, B_HEADS * B_KV_LORA), lambda bi, i, j: (bi, i, 0)),
                  pl.BlockSpec((1, QB, B_HEADS * B_ROPE), lambda bi, i, j: (bi, i, 0)),
                  pl.BlockSpec((1, kb, B_KV_LORA), kidx),
                  pl.BlockSpec((1, kb, B_HEADS * B_ROPE), kidx)],
        out_specs=pl.BlockSpec((1, QB, B_HEADS * B_KV_LORA), lambda bi, i, j: (bi, i, 0)),
        out_shape=jax.ShapeDtypeStruct((b, s, B_HEADS * B_KV_LORA), BF16),
        scratch_shapes=[pltpu.VMEM((B_HEADS * QB, 512), BF16),
                        pltpu.VMEM((B_HEADS * QB, 1), F32),
                        pltpu.VMEM((B_HEADS * QB, 1), F32),
                        pltpu.VMEM((B_HEADS * QB, B_KV_LORA), F32),
                        pltpu.VMEM((B_HEADS * QB, kb), BF16)],
        compiler_params=_params(("parallel", "parallel", "arbitrary")),
        name="mla_prompt",
    )(q_lat, q_rope, ckv, kr_rep)


def _even_merge_kernel(oa_ref, obl_ref, h_ref, wbuv_ref, wout_ref, o_ref):
    obs = [_dot(obl_ref[:, p * 512:(p + 1) * 512], wbuv_ref[p]).astype(BF16) for p in range(B_HEADS // 2)]
    ob = jnp.concatenate(obs, axis=1)
    o_ref[...] = (h_ref[...] + _dot(oa_ref[...], wout_ref[0:A_WIDTH, :])
                  + _dot(ob, wout_ref[A_WIDTH:A_WIDTH + B_WIDTH, :]))


def _even_merge(oa, obl, h, wts):
    n, d = h.shape
    tm = min(ROW_TILE, n)
    row = lambda w: pl.BlockSpec((tm, w), lambda i: (i, 0))
    return pl.pallas_call(
        _even_merge_kernel,
        grid=(n // tm,),
        in_specs=[row(A_WIDTH), row(B_HEADS * B_KV_LORA), row(d),
                  _const_spec(wts["w_buvbd"].shape), _const_spec(wts["w_out_even"].shape)],
        out_specs=row(d),
        out_shape=jax.ShapeDtypeStruct((n, d), F32),
        compiler_params=_params(("parallel",)),
        name="even_merge",
    )(oa, obl, h, wts["w_buvbd"], wts["w_out_even"])


def _tail_kernel(h_ref, p_ref, gffn_ref, gple_ref, wg_ref, wu_ref, wd_ref, wpg_ref, wpp_ref, gfin_ref, o_ref,
                 *, final, fc):
    h = h_ref[...]
    hn = _rms(h, gffn_ref[...]).astype(BF16)
    dff = wg_ref.shape[1]
    acc = jnp.zeros(h.shape, F32)
    for c in range(dff // fc):
        g = _dot(hn, wg_ref[:, c * fc:(c + 1) * fc])
        u = _dot(hn, wu_ref[:, c * fc:(c + 1) * fc])
        a = (g * jax.nn.sigmoid(g) * u).astype(BF16)
        acc = acc + _dot(a, wd_ref[c * fc:(c + 1) * fc, :])
    h2 = h + acc
    gate = jax.nn.sigmoid(_dot(_rms(h2, gple_ref[...]).astype(BF16), wpg_ref[...]))
    h3 = h2 + gate * _dot(p_ref[...].astype(BF16), wpp_ref[...])
    if final:
        h3 = _rms(h3, gfin_ref[...])
    o_ref[...] = h3


def _tail(h, p, lw, g_final, final):
    n, d = h.shape
    tm = min(ROW_TILE, n)
    row = lambda w: pl.BlockSpec((tm, w), lambda i: (i, 0))
    kern = functools.partial(_tail_kernel, final=final, fc=256)
    return pl.pallas_call(
        kern,
        grid=(n // tm,),
        in_specs=[row(d), row(p.shape[1]), _const_spec((1, d)), _const_spec((1, d)),
                  _const_spec(lw["wg"].shape), _const_spec(lw["wu"].shape), _const_spec(lw["wd"].shape),
                  _const_spec(lw["wpg"].shape), _const_spec(lw["wpp"].shape), _const_spec((1, d))],
        out_specs=row(d),
        out_shape=jax.ShapeDtypeStruct((n, d), F32),
        compiler_params=_params(("parallel",)),
        name="layer_tail",
    )(h, p, lw["g_ffn"], lw["g_ple"], lw["wg"], lw["wu"], lw["wd"], lw["wpg"], lw["wpp"], g_final)


def _odd_proj_kernel(x_ref, g_ref, w_ref, q_o, k_o, kb_o, v_o, vb_o):
    hn = _rms(x_ref[...], g_ref[...]).astype(BF16)
    q_o[...] = _dot(hn, w_ref[:, 0:C_WIDTH]).astype(BF16)
    k = _dot(hn, w_ref[:, C_WIDTH:C_WIDTH + C_KV_WIDTH])
    v = _dot(hn, w_ref[:, C_WIDTH + C_KV_WIDTH:C_WIDTH + 2 * C_KV_WIDTH])
    k_o[...] = k
    kb_o[...] = k.astype(BF16)
    v_o[...] = v
    vb_o[...] = v.astype(BF16)


def _odd_proj(x, g, w):
    n, d = x.shape
    tm = min(ROW_TILE, n)
    row = lambda wd: pl.BlockSpec((tm, wd), lambda i: (i, 0))
    outs = [(C_WIDTH, BF16), (C_KV_WIDTH, F32), (C_KV_WIDTH, BF16), (C_KV_WIDTH, F32), (C_KV_WIDTH, BF16)]
    return pl.pallas_call(
        _odd_proj_kernel,
        grid=(n // tm,),
        in_specs=[row(d), _const_spec((1, d)), _const_spec(w.shape)],
        out_specs=[row(wd) for wd, _ in outs],
        out_shape=[jax.ShapeDtypeStruct((n, wd), dt) for wd, dt in outs],
        compiler_params=_params(("parallel",)),
        name="odd_proj",
    )(x, g, w)


def _swa_prompt_kernel(sink_ref, q_ref, kp_ref, kc_ref, vp_ref, vc_ref, bias_ref, h_ref, wout_ref, o_ref):
    i = pl.program_id(1)
    kk = jnp.concatenate([kp_ref[0], kc_ref[0]], axis=0)
    vv = jnp.concatenate([vp_ref[0], vc_ref[0]], axis=0)
    r = lax.broadcasted_iota(jnp.int32, (QB, 2 * WINDOW), 0)
    jj = lax.broadcasted_iota(jnp.int32, (QB, 2 * WINDOW), 1)
    dist = WINDOW + r - jj
    mask = (dist >= 0) & (dist <= WINDOW) & ((jj >= WINDOW) | (i > 0))
    lane = jj // C_HEAD_DIM
    outs = []
    for g in range(C_GROUP):
        qg = q_ref[0, :, g * 256:(g + 1) * 256]
        og = jnp.zeros((QB, 256), F32)
        for k in range(C_KV_HEADS):
            hidx = k * C_GROUP + g
            qm = jnp.where(lane == k, qg, jnp.zeros_like(qg))
            l = _dot_nt(qm, kk) * C_SCALE + bias_ref[hidx]
            l = jnp.where(mask, l, NEG_INF)
            sk = sink_ref[hidx]
            m = jnp.maximum(jnp.max(l, axis=1, keepdims=True), sk)
            e = jnp.exp(l - m)
            den = jnp.sum(e, axis=1, keepdims=True) + jnp.exp(sk - m)
            og = jnp.where(lane == k, _dot((e / den).astype(BF16), vv), og)
        outs.append(og.astype(BF16))
    o_all = jnp.concatenate(outs, axis=1)
    o_ref[0] = h_ref[0] + _dot(o_all, wout_ref[...])


def _swa_prompt(q, k, v, bias_raw, sinks, h, w_out):
    b, s, d = h.shape
    nq = s // QB
    cur = lambda w: pl.BlockSpec((1, QB, w), lambda bi, i: (bi, i, 0))
    prev = lambda w: pl.BlockSpec((1, QB, w), lambda bi, i: (bi, jnp.maximum(i - 1, 0), 0))
    return pl.pallas_call(
        _swa_prompt_kernel,
        grid=(b, nq),
        in_specs=[pl.BlockSpec(memory_space=pltpu.SMEM),
                  cur(C_WIDTH), prev(C_KV_WIDTH), cur(C_KV_WIDTH), prev(C_KV_WIDTH), cur(C_KV_WIDTH),
                  _const_spec(bias_raw.shape), cur(d), _const_spec(w_out.shape)],
        out_specs=cur(d),
        out_shape=jax.ShapeDtypeStruct((b, s, d), F32),
        compiler_params=_params(("parallel", "parallel")),
        name="swa_prompt",
    )(sinks, q, k, k, v, v, bias_raw, h, w_out)


def _rel_bucket(dist):
    n = jnp.maximum(dist, 0)
    max_exact = REL_BUCKETS // 2
    nf = jnp.maximum(n, 1).astype(F32)
    large = max_exact + (jnp.log(nf / max_exact) / math.log(REL_MAX_DIST / max_exact)
                         * (REL_BUCKETS - max_exact)).astype(jnp.int32)
    large = jnp.minimum(large, REL_BUCKETS - 1)
    return jnp.where(n < max_exact, n, large)


def _bias_window(rel_table):
    r = jnp.arange(QB)[:, None]
    j = jnp.arange(2 * QB)[None, :]
    return jnp.transpose(rel_table[_rel_bucket(WINDOW + r - j)].astype(F32), (2, 0, 1))


def _rope_tables(pos, reps):
    half = B_ROPE // 2
    inv = ROPE_THETA ** (-jnp.arange(half, dtype=F32) / half)
    ang = pos.astype(F32)[:, None] * inv
    cos, sin = jnp.cos(ang), jnp.sin(ang)
    cos_f = jnp.tile(jnp.concatenate([cos, cos], axis=1), (1, reps))
    sin_f = jnp.tile(jnp.concatenate([-sin, sin], axis=1), (1, reps))
    return cos_f, sin_f


def _block_diag_pairs(m):
    z = jnp.zeros_like(m[0])
    return jnp.stack([jnp.concatenate([jnp.concatenate([m[2 * p], z], axis=1),
                                       jnp.concatenate([z, m[2 * p + 1]], axis=1)], axis=0)
                      for p in range(m.shape[0] // 2)])


def _prep_even_weights(w_in, w_out, g_bq, w_buq, g_bkv, w_buk, w_buv):
    d = w_in.shape[0]
    splits = np.cumsum([A_WIDTH, A_WIDTH, A_WIDTH, IDX_WIDTH, IDX_DIM, IDX_HEADS, B_Q_LORA, B_KV_LORA])
    qa, ka, va, qi, ki, wi, cq, ckv, kr = jnp.split(w_in, splits.tolist(), axis=1)
    wi_pad = jnp.concatenate([wi, jnp.zeros((d, LANE - IDX_HEADS), w_in.dtype)], axis=1)
    w_all = jnp.concatenate([qa, ka, va, qi, cq, ckv, ki, ki, jnp.tile(kr, (1, B_HEADS)), wi_pad], axis=1)
    wq = w_buq.reshape(B_Q_LORA, B_HEADS, B_NOPE + B_ROPE)
    w_qn = wq[:, :, :B_NOPE].reshape(B_Q_LORA, B_HEADS * B_NOPE)
    w_qr = wq[:, :, B_NOPE:].reshape(B_Q_LORA, B_HEADS * B_ROPE)
    w_bukt = jnp.transpose(w_buk, (1, 2, 0))
    w_buvh = jnp.transpose(w_buv, (1, 0, 2))
    return {
        "w_in": w_all.astype(BF16),
        "g_bq": g_bq.reshape(1, -1), "g_bkv": g_bkv.reshape(1, -1),
        "w_qn": w_qn.astype(BF16), "w_qr": w_qr.astype(BF16),
        "w_bukbd": _block_diag_pairs(w_bukt).astype(BF16),
        "w_buvbd": _block_diag_pairs(w_buvh).astype(BF16),
        "w_out_even": w_out.astype(BF16),
    }


def _prep_odd_weights(w_in, w_out):
    d = w_in.shape[0]
    q = w_in[:, :C_WIDTH].reshape(d, C_KV_HEADS, C_GROUP, C_HEAD_DIM)
    q = jnp.transpose(q, (0, 2, 1, 3)).reshape(d, C_WIDTH)
    w_in_p = jnp.concatenate([q, w_in[:, C_WIDTH:]], axis=1).astype(BF16)
    wo = w_out.reshape(C_KV_HEADS, C_GROUP, C_HEAD_DIM, -1)
    wo = jnp.transpose(wo, (1, 0, 2, 3)).reshape(C_WIDTH, -1).astype(BF16)
    return w_in_p, wo


QP = 8
GB = LANE // QP


def _pages_per_step(n_pages):
    return 8 if n_pages % 8 == 0 else n_pages


def _page_index(b, j, pt_ref, *, p, pps):
    return (0, pt_ref[b, j * pps + p], 0, 0)


def _page_specs(width, pps):
    return [pl.BlockSpec((None, None, PAGE, width), functools.partial(_page_index, p=p, pps=pps))
            for p in range(pps)]


def _new_key_ok(rows, bloc, t):
    row = lax.broadcasted_iota(jnp.int32, (rows, LANE), 0)
    col = lax.broadcasted_iota(jnp.int32, (rows, LANE), 1)
    return (col // QP == bloc) & (col % QP < t) & (col % QP <= row % QP)


def _online_update(s, v, m_scr, l_scr, acc):
    m_old = m_scr[...]
    m_new = jnp.maximum(m_old, jnp.max(s, axis=1, keepdims=True))
    m_safe = jnp.where(m_new == NEG_INF, 0.0, m_new)
    alpha = jnp.exp(m_old - m_safe)
    p = jnp.exp(s - m_safe)
    l_scr[...] = alpha * l_scr[...] + jnp.sum(p, axis=1, keepdims=True)
    acc[...] = alpha * acc[...] + _dot(p.astype(BF16), v)
    m_scr[...] = m_new


def _idx_scores_kernel(pt_ref, q_ref, w_ref, *rest, pps):
    pages, o_ref, kcat = rest[:pps], rest[pps], rest[pps + 1]
    for p in range(pps):
        kcat[p * PAGE:(p + 1) * PAGE, :] = pages[p][...].astype(BF16)
    d = _dot_nt(q_ref[0], kcat[...])
    for p in range(pps):
        cols = slice(p * PAGE, (p + 1) * PAGE)
        a = w_ref[0, 0] * jnp.maximum(d[0:QP, cols], 0.0)
        for h in range(1, IDX_HEADS):
            a = a + w_ref[0, h] * jnp.maximum(d[h * QP:(h + 1) * QP, cols], 0.0)
        o_ref[0, :, cols] = a


def _idx_scores_sample(page_table, qi_s, w_s, cache_idx):
    bd, n_pages = page_table.shape
    pps = _pages_per_step(n_pages)
    kern = functools.partial(_idx_scores_kernel, pps=pps)
    grid_spec = pltpu.PrefetchScalarGridSpec(
        num_scalar_prefetch=1, grid=(bd, n_pages // pps),
        in_specs=[pl.BlockSpec((1, IDX_HEADS * QP, IDX_DIM), lambda b, j, pt: (b, 0, 0)),
                  pl.BlockSpec((1, IDX_HEADS, QP, LANE), lambda b, j, pt: (b, 0, 0, 0))]
                 + _page_specs(IDX_DIM, pps),
        out_specs=pl.BlockSpec((1, QP, pps * PAGE), lambda b, j, pt: (b, 0, j)),
        scratch_shapes=[pltpu.VMEM((pps * PAGE, IDX_DIM), BF16)])
    return pl.pallas_call(
        kern, grid_spec=grid_spec,
        out_shape=jax.ShapeDtypeStruct((bd, QP, n_pages * PAGE), F32),
        compiler_params=_params(("parallel", "arbitrary")),
        name="idx_scores_sample",
    )(page_table, qi_s, w_s, *([cache_idx] * pps))


def _select_sample_kernel(sc_ref, qi_ref, wi_ref, kin_ref, mbp_ref, mbn_ref, keybuf, maskb, cbuf,
                          *, n_sel, idx_bits, npast, t):
    rows = sc_ref.shape[0]
    row = lax.broadcasted_iota(jnp.int32, (rows, LANE), 0)
    lane1 = lax.broadcasted_iota(jnp.int32, (rows, LANE), 1)
    real = (row % QP) < t
    for c in range(npast):
        keybuf[c] = jnp.where(real, _sortable_key(sc_ref[:, c * LANE:(c + 1) * LANE]), KEY_NEG)
    a = jnp.zeros((rows, LANE), F32)
    for h in range(IDX_HEADS):
        blk = qi_ref[:, (h // 2) * LANE:(h // 2 + 1) * LANE]
        qm = jnp.where((lane1 // IDX_DIM) == (h % 2), blk, jnp.zeros_like(blk))
        a = a + wi_ref[:, h:h + 1] * jnp.maximum(_dot_nt(qm, kin_ref[...]), 0.0)
    ok = jnp.logical_and(_new_key_ok(rows, row // QP, t), real)
    keybuf[npast] = jnp.where(ok, _sortable_key(a), KEY_NEG)
    keybuf[npast + 1] = jnp.full((rows, LANE), KEY_NEG, jnp.int32)
    _topk_mask(keybuf, maskb, cbuf, (npast + 2) // 2, n_sel, idx_bits)
    for c in range(npast):
        mbp_ref[:, c * LANE:(c + 1) * LANE] = maskb[c]
    mbn_ref[...] = maskb[npast]


def _select_sample(scores, qi8, wi8, kin8, t):
    rows, past = scores.shape
    npast = past // LANE
    assert npast % 2 == 0
    n_sel = min(TOPK_MAX, (past + t) // 4)
    idx_bits = int(math.ceil(math.log2((npast + 2) * LANE))) + 1
    kern = functools.partial(_select_sample_kernel, n_sel=n_sel, idx_bits=idx_bits, npast=npast, t=t)
    rb = lambda w: pl.BlockSpec((LANE, w), lambda i: (i, 0))
    return pl.pallas_call(
        kern, grid=(rows // LANE,),
        in_specs=[rb(past), rb(IDX_WIDTH), rb(IDX_HEADS), rb(2 * IDX_DIM)],
        out_specs=[rb(past), rb(LANE)],
        out_shape=[jax.ShapeDtypeStruct((rows, past), F32), jax.ShapeDtypeStruct((rows, LANE), F32)],
        scratch_shapes=[pltpu.VMEM((npast + 2, LANE, LANE), jnp.int32),
                        pltpu.VMEM((npast + 2, LANE, LANE), F32),
                        pltpu.VMEM((LANE, LANE), jnp.int32)],
        compiler_params=_params(("parallel",)),
        name="select_sample",
    )(scores, qi8, wi8, kin8)


def _attn_a_sample_kernel(pt_ref, q_ref, mb_ref, bias_ref, mbn_ref, biasn_ref, kn_ref, vn_ref, *rest, pps):
    kpages, vpages, o_ref = rest[:pps], rest[pps:2 * pps], rest[2 * pps]
    kcat, vcat, m_scr, l_scr, acc = rest[2 * pps + 1:]
    j = pl.program_id(1)
    rows = A_HEADS * QP

    @pl.when(j == 0)
    def _():
        m_scr[...] = jnp.full(m_scr.shape, NEG_INF, F32)
        l_scr[...] = jnp.zeros(l_scr.shape, F32)
        acc[...] = jnp.zeros(acc.shape, F32)

    for p in range(pps):
        kcat[p * PAGE:(p + 1) * PAGE, :] = kpages[p][...].astype(BF16)
        vcat[p * PAGE:(p + 1) * PAGE, :] = vpages[p][...].astype(BF16)
    n = pps * PAGE
    q = q_ref[0]
    s = _dot_nt(q, kcat[...]) * A_SCALE + bias_ref[0]
    s = (s.reshape(A_HEADS, QP, n) + mb_ref[0][None]).reshape(rows, n)
    _online_update(s, vcat[...], m_scr, l_scr, acc)

    @pl.when(j == pl.num_programs(1) - 1)
    def _():
        sn = _dot_nt(q, kn_ref[...]) * A_SCALE + biasn_ref[...]
        sn = (sn.reshape(A_HEADS, QP, LANE) + mbn_ref[...][None]).reshape(rows, LANE)
        _online_update(sn, vn_ref[...], m_scr, l_scr, acc)
        l = l_scr[...]
        o = acc[...] / jnp.where(l == 0.0, 1.0, l)
        lane = lax.broadcasted_iota(jnp.int32, (QP, A_WIDTH), 1) // A_HEAD_DIM
        out = jnp.zeros((QP, A_WIDTH), F32)
        for h in range(A_HEADS):
            out = jnp.where(lane == h, o[h * QP:(h + 1) * QP, :], out)
        o_ref[0] = out.astype(BF16)


def _attn_a_sample(page_table, qa_s, mb_past, bias_pages, mb_new, bias_new, kn8, vn8, cache_k, cache_v):
    bd, n_pages = page_table.shape
    pps = _pages_per_step(n_pages)
    n = pps * PAGE
    rows = A_HEADS * QP
    kern = functools.partial(_attn_a_sample_kernel, pps=pps)
    grid_spec = pltpu.PrefetchScalarGridSpec(
        num_scalar_prefetch=1, grid=(bd, n_pages // pps),
        in_specs=[pl.BlockSpec((1, rows, A_WIDTH), lambda b, j, pt: (b, 0, 0)),
                  pl.BlockSpec((1, QP, n), lambda b, j, pt: (b, 0, j)),
                  pl.BlockSpec((1, rows, n), lambda b, j, pt: (j, 0, 0)),
                  pl.BlockSpec((QP, LANE), lambda b, j, pt: (b, 0)),
                  pl.BlockSpec((rows, LANE), lambda b, j, pt: (0, 0)),
                  pl.BlockSpec((LANE, A_WIDTH), lambda b, j, pt: (b // GB, 0)),
                  pl.BlockSpec((LANE, A_WIDTH), lambda b, j, pt: (b // GB, 0))]
                 + _page_specs(A_WIDTH, pps) + _page_specs(A_WIDTH, pps),
        out_specs=pl.BlockSpec((1, QP, A_WIDTH), lambda b, j, pt: (b, 0, 0)),
        scratch_shapes=[pltpu.VMEM((n, A_WIDTH), BF16), pltpu.VMEM((n, A_WIDTH), BF16),
                        pltpu.VMEM((rows, 1), F32), pltpu.VMEM((rows, 1), F32), pltpu.VMEM((rows, A_WIDTH), F32)])
    return pl.pallas_call(
        kern, grid_spec=grid_spec,
        out_shape=jax.ShapeDtypeStruct((bd, QP, A_WIDTH), BF16),
        compiler_params=_params(("parallel", "arbitrary")),
        name="attn_a_sample",
    )(page_table, qa_s, mb_past, bias_pages, mb_new, bias_new, kn8, vn8, *([cache_k] * pps), *([cache_v] * pps))


def _mla_sample_kernel(pt_ref, ql_ref, qr_ref, cn_ref, rn_ref, *rest, pps, t):
    cpages, rpages, o_ref = rest[:pps], rest[pps:2 * pps], rest[2 * pps]
    ccat, rcat, m_scr, l_scr, acc = rest[2 * pps + 1:]
    b = pl.program_id(0)
    j = pl.program_id(1)
    rows = B_HEADS * QP

    @pl.when(j == 0)
    def _():
        m_scr[...] = jnp.full(m_scr.shape, NEG_INF, F32)
        l_scr[...] = jnp.zeros(l_scr.shape, F32)
        acc[...] = jnp.zeros(acc.shape, F32)

    for p in range(pps):
        ccat[p * PAGE:(p + 1) * PAGE, :] = cpages[p][...].astype(BF16)
        rcat[p * PAGE:(p + 1) * PAGE, :] = rpages[p][...].astype(BF16)
    ql, qr = ql_ref[0], qr_ref[0]
    s = (_dot_nt(ql, ccat[...]) + _dot_nt(qr, rcat[...])) * MLA_SCALE
    _online_update(s, ccat[...], m_scr, l_scr, acc)

    @pl.when(j == pl.num_programs(1) - 1)
    def _():
        cn = cn_ref[...]
        sn = (_dot_nt(ql, cn) + _dot_nt(qr, rn_ref[...])) * MLA_SCALE
        sn = jnp.where(_new_key_ok(rows, b % GB, t), sn, NEG_INF)
        _online_update(sn, cn, m_scr, l_scr, acc)
        o_ref[0] = (acc[...] / l_scr[...]).astype(BF16)


def _mla_sample(page_table, ql_s, qr_s, cn8, rn8, cache_ckv, cache_kr, t):
    bd, n_pages = page_table.shape
    pps = _pages_per_step(n_pages)
    n = pps * PAGE
    rows = B_HEADS * QP
    kern = functools.partial(_mla_sample_kernel, pps=pps, t=t)
    grid_spec = pltpu.PrefetchScalarGridSpec(
        num_scalar_prefetch=1, grid=(bd, n_pages // pps),
        in_specs=[pl.BlockSpec((1, rows, B_KV_LORA), lambda b, j, pt: (b, 0, 0)),
                  pl.BlockSpec((1, rows, B_ROPE), lambda b, j, pt: (b, 0, 0)),
                  pl.BlockSpec((LANE, B_KV_LORA), lambda b, j, pt: (b // GB, 0)),
                  pl.BlockSpec((LANE, B_ROPE), lambda b, j, pt: (b // GB, 0))]
                 + _page_specs(B_KV_LORA, pps) + _page_specs(B_ROPE, pps),
        out_specs=pl.BlockSpec((1, rows, B_KV_LORA), lambda b, j, pt: (b, 0, 0)),
        scratch_shapes=[pltpu.VMEM((n, B_KV_LORA), BF16), pltpu.VMEM((n, B_ROPE), BF16),
                        pltpu.VMEM((rows, 1), F32), pltpu.VMEM((rows, 1), F32), pltpu.VMEM((rows, B_KV_LORA), F32)])
    return pl.pallas_call(
        kern, grid_spec=grid_spec,
        out_shape=jax.ShapeDtypeStruct((bd, rows, B_KV_LORA), BF16),
        compiler_params=_params(("parallel", "arbitrary")),
        name="mla_sample",
    )(page_table, ql_s, qr_s, cn8, rn8, *([cache_ckv] * pps), *([cache_kr] * pps))


def _swa_sample_kernel(q_ref, bk_ref, bv_ref, kn_ref, vn_ref, ks_ref, vs_ref, bb_ref, bn_ref, sk_ref,
                       o_ref, nk_ref, nv_ref, *, t):
    rows = C_HEADS * QP
    row = lax.broadcasted_iota(jnp.int32, (rows, LANE), 0)
    col = lax.broadcasted_iota(jnp.int32, (rows, LANE), 1)
    buf_ok = col >= row % QP
    lane = lax.broadcasted_iota(jnp.int32, (QP, C_KV_WIDTH), 1) // C_HEAD_DIM
    row8 = lax.broadcasted_iota(jnp.int32, (QP, C_KV_WIDTH), 0)
    sk = sk_ref[:, 0:1]
    kn, vn = kn_ref[...], vn_ref[...]
    for bi in range(GB):
        q = q_ref[bi]
        bk, bv = bk_ref[bi], bv_ref[bi]
        lb = jnp.where(buf_ok, _dot_nt(q, bk.astype(BF16)) * C_SCALE + bb_ref[...], NEG_INF)
        ln = jnp.where(_new_key_ok(rows, bi, t), _dot_nt(q, kn) * C_SCALE + bn_ref[...], NEG_INF)
        m = jnp.maximum(jnp.maximum(jnp.max(lb, axis=1, keepdims=True), jnp.max(ln, axis=1, keepdims=True)), sk)
        eb = jnp.exp(lb - m)
        en = jnp.exp(ln - m)
        den = jnp.sum(eb, axis=1, keepdims=True) + jnp.sum(en, axis=1, keepdims=True) + jnp.exp(sk - m)
        o = _dot((eb / den).astype(BF16), bv.astype(BF16)) + _dot((en / den).astype(BF16), vn)
        for g in range(C_GROUP):
            og = jnp.zeros((QP, C_KV_WIDTH), F32)
            for k in range(C_KV_HEADS):
                hidx = k * C_GROUP + g
                og = jnp.where(lane == k, o[hidx * QP:(hidx + 1) * QP, :], og)
            o_ref[bi, :, g * C_KV_WIDTH:(g + 1) * C_KV_WIDTH] = og.astype(BF16)
        for src, shifted, dst in ((bk, ks_ref, nk_ref), (bv, vs_ref, nv_ref)):
            rolled = pltpu.roll(src, WINDOW - t, 0)
            dst[bi, 0:WINDOW - QP, :] = rolled[0:WINDOW - QP, :]
            dst[bi, WINDOW - QP:WINDOW, :] = jnp.where(row8 < QP - t, rolled[WINDOW - QP:WINDOW, :], shifted[bi])


def _swa_sample(q_rows, buf_k, buf_v, kn8, vn8, k_shift, v_shift, bias_buf, bias_new, sink_b, t):
    bd = q_rows.shape[0]
    rows = C_HEADS * QP
    kern = functools.partial(_swa_sample_kernel, t=t)
    g3 = lambda r, w: pl.BlockSpec((GB, r, w), lambda i: (i, 0, 0))
    g2 = lambda w: pl.BlockSpec((LANE, w), lambda i: (i, 0))
    return pl.pallas_call(
        kern, grid=(bd // GB,),
        in_specs=[g3(rows, C_KV_WIDTH), g3(WINDOW, C_KV_WIDTH), g3(WINDOW, C_KV_WIDTH), g2(C_KV_WIDTH), g2(C_KV_WIDTH),
                  g3(QP, C_KV_WIDTH), g3(QP, C_KV_WIDTH),
                  _const_spec((rows, LANE)), _const_spec((rows, LANE)), _const_spec((rows, LANE))],
        out_specs=[g3(QP, C_WIDTH), g3(WINDOW, C_KV_WIDTH), g3(WINDOW, C_KV_WIDTH)],
        out_shape=[jax.ShapeDtypeStruct((bd, QP, C_WIDTH), BF16),
                   jax.ShapeDtypeStruct((bd, WINDOW, C_KV_WIDTH), F32),
                   jax.ShapeDtypeStruct((bd, WINDOW, C_KV_WIDTH), F32)],
        compiler_params=_params(("parallel",)),
        name="swa_sample",
    )(q_rows, buf_k, buf_v, kn8, vn8, k_shift, v_shift, bias_buf, bias_new, sink_b)


def _pad_q(a, t):
    pad = [(0, 0), (0, QP - t)] + [(0, 0)] * (a.ndim - 2)
    return jnp.pad(a, pad)


def _sample_even_attention(se, bd, t, cache_a_k, cache_a_v, cache_a_idx, cache_b_ckv, cache_b_krope,
                           page_table, bias_rel):
    n_pages = page_table.shape[1]
    pps = _pages_per_step(n_pages)
    nj = n_pages // pps
    r = lambda a: a.reshape(bd, t, -1)
    qi = r(se["qi"])
    qi_s = _pad_q(jnp.transpose(qi.reshape(bd, t, IDX_HEADS, IDX_DIM), (0, 2, 1, 3)).reshape(bd * IDX_HEADS, t, IDX_DIM), t)
    qi_s = qi_s.reshape(bd, IDX_HEADS * QP, IDX_DIM)
    wi = r(se["wi"])
    w_s = _pad_q(jnp.transpose(wi, (0, 2, 1)).reshape(bd * IDX_HEADS, t), t).reshape(bd, IDX_HEADS, QP, 1)
    w_s = jnp.broadcast_to(w_s, (bd, IDX_HEADS, QP, LANE))
    scores = _idx_scores_sample(page_table, qi_s, w_s, cache_a_idx)
    qi8 = _pad_q(qi, t).reshape(bd * QP, IDX_WIDTH)
    wi8 = _pad_q(wi, t).reshape(bd * QP, IDX_HEADS)
    kin8 = _pad_q(r(se["ki2"]), t).reshape(bd * QP, 2 * IDX_DIM)
    mb_past, mb_new = _select_sample(scores.reshape(bd * QP, -1), qi8, wi8, kin8, t)
    head_of_lane = jnp.arange(A_WIDTH) // A_HEAD_DIM
    qa = _pad_q(r(se["qa"]), t)
    qa_s = jnp.where(head_of_lane[None, None, None, :] == jnp.arange(A_HEADS)[None, :, None, None],
                     qa[:, None, :, :], jnp.zeros((), qa.dtype)).reshape(bd, A_HEADS * QP, A_WIDTH)
    near = bias_rel[:, :QP, :]
    bias_pages = jnp.zeros((nj, A_HEADS * QP, pps * PAGE), F32)
    bias_pages = bias_pages.at[nj - 1, :, (pps - 1) * PAGE:].set(near[:, :, :PAGE].reshape(A_HEADS * QP, PAGE))
    bias_new = jnp.tile(near[:, :, PAGE:PAGE + QP], (1, 1, GB)).reshape(A_HEADS * QP, LANE)
    kn8 = _pad_q(r(se["ka_bf"]), t).reshape(bd * QP, A_WIDTH)
    vn8 = _pad_q(r(se["va_bf"]), t).reshape(bd * QP, A_WIDTH)
    n_pool = cache_a_k.shape[1]
    oa = _attn_a_sample(page_table, qa_s, mb_past.reshape(bd, QP, -1), bias_pages, mb_new, bias_new, kn8, vn8,
                        cache_a_k.reshape(1, n_pool, PAGE, A_WIDTH), cache_a_v.reshape(1, n_pool, PAGE, A_WIDTH))
    oa = oa[:, :t, :].reshape(bd * t, A_WIDTH)
    hq = lambda a, w: _pad_q(jnp.transpose(a.reshape(bd, t, B_HEADS, w), (0, 2, 1, 3)).reshape(bd * B_HEADS, t, w), t
                             ).reshape(bd, B_HEADS * QP, w)
    ql_s = hq(r(se["q_lat"]), B_KV_LORA)
    qr_s = hq(r(se["q_rope"]), B_ROPE)
    cn8 = _pad_q(r(se["ckv_bf"]), t).reshape(bd * QP, B_KV_LORA)
    rn8 = _pad_q(r(se["kr"]).astype(BF16), t).reshape(bd * QP, B_ROPE)
    ob = _mla_sample(page_table, ql_s, qr_s, cn8, rn8, cache_b_ckv, cache_b_krope, t)
    ob = jnp.transpose(ob.reshape(bd, B_HEADS, QP, B_KV_LORA)[:, :, :t, :], (0, 2, 1, 3))
    return oa, ob.reshape(bd * t, B_HEADS * B_KV_LORA)


def _sample_odd_attention(q, k, kb, v, vb, buf_k, buf_v, sinks, bias_raw, bd, t):
    assert buf_k.shape[1] == WINDOW
    q5 = _pad_q(q.reshape(bd, t, C_GROUP, C_KV_HEADS, C_HEAD_DIM), t)
    q5 = jnp.transpose(q5, (0, 3, 2, 1, 4))
    kv_of_lane = jnp.arange(C_KV_WIDTH) // C_HEAD_DIM
    q_rows = jnp.where(kv_of_lane[None, None, None, None, :] == jnp.arange(C_KV_HEADS)[None, :, None, None, None],
                       jnp.tile(q5, (1, 1, 1, 1, C_KV_HEADS)), jnp.zeros((), q.dtype))
    q_rows = q_rows.reshape(bd, C_HEADS * QP, C_KV_WIDTH)
    near = bias_raw[:, :QP, :]
    bias_buf = near[:, :, :WINDOW].reshape(C_HEADS * QP, WINDOW)
    bias_new = jnp.tile(near[:, :, WINDOW:WINDOW + QP], (1, 1, GB)).reshape(C_HEADS * QP, LANE)
    sink_b = jnp.broadcast_to(jnp.repeat(sinks.astype(F32), QP)[:, None], (C_HEADS * QP, LANE))
    r = lambda a: a.reshape(bd, t, C_KV_WIDTH)
    kn8 = _pad_q(r(kb), t).reshape(bd * QP, C_KV_WIDTH)
    vn8 = _pad_q(r(vb), t).reshape(bd * QP, C_KV_WIDTH)
    shift = lambda a: jnp.pad(r(a), [(0, 0), (QP - t, 0), (0, 0)])
    o8, nk, nv = _swa_sample(q_rows, buf_k.reshape(bd, WINDOW, C_KV_WIDTH), buf_v.reshape(bd, WINDOW, C_KV_WIDTH),
                             kn8, vn8, shift(k), shift(v), bias_buf, bias_new, sink_b, t)
    o = o8[:, :t, :].reshape(bd * t, C_WIDTH)
    return o, nk.reshape(bd, WINDOW, C_KV_HEADS, C_HEAD_DIM), nv.reshape(bd, WINDOW, C_KV_HEADS, C_HEAD_DIM)


def _out_proj_kernel(o_ref, h_ref, w_ref, out_ref):
    out_ref[...] = h_ref[...] + _dot(o_ref[...], w_ref[...])


def _out_proj(o, h, w):
    n, d = h.shape
    tm = min(ROW_TILE, n)
    row = lambda wd: pl.BlockSpec((tm, wd), lambda i: (i, 0))
    return pl.pallas_call(
        _out_proj_kernel,
        grid=(n // tm,),
        in_specs=[row(o.shape[1]), row(d), _const_spec(w.shape)],
        out_specs=row(d),
        out_shape=jax.ShapeDtypeStruct((n, d), F32),
        compiler_params=_params(("parallel",)),
        name="out_proj",
    )(o, h, w)


def kernel(x_prompt, x_sample, cache_a_k, cache_a_v, cache_a_idx, cache_b_ckv, cache_b_krope, state_c_k, state_c_v, page_table, p_prompt, p_sample, rel_table, w_in_even, w_out_even, g_bq, w_buq, g_bkv, w_buk, w_buv, w_in_odd, w_out_odd, c_sinks, g_mix, g_ffn, w_ffn_gate, w_ffn_up, w_ffn_down, g_ple, w_ple_gate, w_ple_proj, g_final):
    b, s, d = x_prompt.shape
    bd, t, _ = x_sample.shape
    depth = g_mix.shape[0]
    assert depth == 2 and w_in_even.shape[0] == 1 and w_in_odd.shape[0] == 1
    assert s % 256 == 0 and rel_table.shape == (REL_BUCKETS, A_HEADS)
    assert t <= QP and bd % GB == 0
    past = page_table.shape[1] * PAGE

    we = _prep_even_weights(w_in_even[0], w_out_even[0], g_bq[0], w_buq[0], g_bkv[0], w_buk[0], w_buv[0])
    w_in_o, w_out_o = _prep_odd_weights(w_in_odd[0], w_out_odd[0])
    lws = [{"g_ffn": g_ffn[i].reshape(1, d), "g_ple": g_ple[i].reshape(1, d),
            "wg": w_ffn_gate[i].astype(BF16), "wu": w_ffn_up[i].astype(BF16), "wd": w_ffn_down[i].astype(BF16),
            "wpg": w_ple_gate[i].astype(BF16), "wpp": w_ple_proj[i].astype(BF16)} for i in range(depth)]
    gfin = g_final.reshape(1, d)
    bias_raw = _bias_window(rel_table)
    bias_rel = bias_raw - rel_table[REL_BUCKETS - 1].astype(F32)[:, None, None]
    cos_p, sin_p = _rope_tables(jnp.arange(s, dtype=jnp.int32), B_HEADS)
    cos_s, sin_s = _rope_tables(jnp.tile(past + jnp.arange(t, dtype=jnp.int32), bd), B_HEADS)

    hp = x_prompt.reshape(b * s, d)
    hs = x_sample.reshape(bd * t, d)
    g0 = g_mix[0].reshape(1, d)
    g1 = g_mix[1].reshape(1, d)

    pe = _even_proj(hp, g0, we, cos_p, sin_p)
    r3 = lambda a: a.reshape(b, s, a.shape[-1])
    oa = _attn_a_prompt(r3(pe["qi"]), r3(pe["wi"]), r3(pe["qa"]), r3(pe["ki2"]), r3(pe["ka_bf"]), r3(pe["va_bf"]),
                        bias_rel)
    obl = _mla_prompt(r3(pe["q_lat"]), r3(pe["q_rope"]), r3(pe["ckv_bf"]), r3(pe["kr_rep"]))
    hp = _even_merge(oa.reshape(b * s, A_WIDTH), obl.reshape(b * s, -1), hp, we)
    hp = _tail(hp, p_prompt[0].reshape(b * s, -1), lws[0], gfin, False)

    se = _even_proj(hs, g0, we, cos_s, sin_s)
    oa_s, obl_s = _sample_even_attention(se, bd, t, cache_a_k, cache_a_v, cache_a_idx, cache_b_ckv, cache_b_krope,
                                         page_table, bias_rel)
    hs = _even_merge(oa_s, obl_s, hs, we)
    hs = _tail(hs, p_sample[0].reshape(bd * t, -1), lws[0], gfin, False)

    q, k, kb, v, vb = _odd_proj(hp, g1, w_in_o)
    hp = _swa_prompt(q.reshape(b, s, -1), kb.reshape(b, s, -1), vb.reshape(b, s, -1), bias_raw, c_sinks[0],
                     hp.reshape(b, s, d), w_out_o).reshape(b * s, d)
    y_prompt = _tail(hp, p_prompt[1].reshape(b * s, -1), lws[1], gfin, True)
    wp = min(WINDOW, s)
    pc_k = k.reshape(b, s, C_KV_HEADS, C_HEAD_DIM)[:, s - wp:]
    pc_v = v.reshape(b, s, C_KV_HEADS, C_HEAD_DIM)[:, s - wp:]

    qs, ks, ksb, vs, vsb = _odd_proj(hs, g1, w_in_o)
    os_, sc_k, sc_v = _sample_odd_attention(qs, ks, ksb, vs, vsb, state_c_k[0], state_c_v[0], c_sinks[0], bias_raw,
                                            bd, t)
    hs = _out_proj(os_, hs, w_out_o)
    y_sample = _tail(hs, p_sample[1].reshape(bd * t, -1), lws[1], gfin, True)

    hd = (A_HEADS, A_HEAD_DIM)
    return (y_prompt.reshape(b, s, d), y_sample.reshape(bd, t, d),
            pe["ka"].reshape(1, b, s, *hd), pe["va"].reshape(1, b, s, *hd), pe["ki"].reshape(1, b, s, IDX_DIM),
            pe["ckv"].reshape(1, b, s, B_KV_LORA), pe["kr"].reshape(1, b, s, B_ROPE),
            pc_k[None], pc_v[None],
            se["ka"].reshape(1, bd, t, *hd), se["va"].reshape(1, bd, t, *hd), se["ki"].reshape(1, bd, t, IDX_DIM),
            se["ckv"].reshape(1, bd, t, B_KV_LORA), se["kr"].reshape(1, bd, t, B_ROPE),
            sc_k[None], sc_v[None])
```

```python
import functools
import math

import numpy as np
import jax
import jax.numpy as jnp
from jax import lax
from jax.experimental import pallas as pl
from jax.experimental.pallas import tpu as pltpu

F32 = jnp.float32
BF16 = jnp.bfloat16

RMS_EPS = 1e-6
A_HEADS, A_HEAD_DIM = 16, 32
A_WIDTH = A_HEADS * A_HEAD_DIM
IDX_HEADS, IDX_DIM = 8, 64
IDX_WIDTH = IDX_HEADS * IDX_DIM
IDX_SCALE = IDX_WIDTH ** -0.5
TOPK_MAX = 256
B_HEADS, B_NOPE, B_ROPE, B_V = 8, 64, 32, 64
B_Q_LORA, B_KV_LORA = 256, 256
B_WIDTH = B_HEADS * B_V
MLA_SCALE = (B_NOPE + B_ROPE) ** -0.5
ROPE_THETA = 10000.0
C_HEADS, C_KV_HEADS, C_HEAD_DIM = 16, 4, 64
C_GROUP = C_HEADS // C_KV_HEADS
C_WIDTH = C_HEADS * C_HEAD_DIM
C_KV_WIDTH = C_KV_HEADS * C_HEAD_DIM
WINDOW = 128
C_SCALE = C_HEAD_DIM ** -0.5
A_SCALE = A_HEAD_DIM ** -0.5
REL_BUCKETS, REL_MAX_DIST = 32, 128
PAGE = 128
QB = 128
LANE = 128
MLA_KB = 512
ROW_TILE = 256
VMEM_LIMIT = 56 * 1024 * 1024

INT_MIN = np.int32(-2 ** 31)
KEY_NEG = np.int32(-2139095041)
NEG_INF = float("-inf")

_NT = (((1,), (1,)), ((), ()))


def _dot(a, b):
    return jnp.dot(a, b, preferred_element_type=F32)


def _dot_nt(a, b):
    return lax.dot_general(a, b, _NT, preferred_element_type=F32)


def _rms(x, g):
    ms = jnp.mean(x * x, axis=-1, keepdims=True)
    return x * lax.rsqrt(ms + RMS_EPS) * g


def _const_spec(shape):
    nd = len(shape)
    return pl.BlockSpec(shape, lambda *_: (0,) * nd, pipeline_mode=pl.Buffered(1))


def _params(sem):
    return pltpu.CompilerParams(dimension_semantics=sem, vmem_limit_bytes=VMEM_LIMIT)


def _rope_apply(x, cos, sin_signed):
    n = x.shape[1]
    lane = lax.broadcasted_iota(jnp.int32, x.shape, 1)
    partner = jnp.where((lane % B_ROPE) < (B_ROPE // 2),
                        pltpu.roll(x, n - B_ROPE // 2, 1), pltpu.roll(x, B_ROPE // 2, 1))
    return x * cos + partner * sin_signed


def _even_proj_kernel(x_ref, g_ref, w_ref, gq_ref, wn_ref, wr_ref, wk_ref, gkv_ref, cos_ref, sin_ref,
                      qa_o, ka_o, kab_o, va_o, vab_o, qi_o, ki_o, ki2_o, wi_o, ql_o, qr_o,
                      ckv_o, ckvb_o, kr_o, krr_o):
    hn = _rms(x_ref[...], g_ref[...]).astype(BF16)
    aw = A_WIDTH
    qa_o[...] = _dot(hn, w_ref[:, 0:aw]).astype(BF16)
    ka = _dot(hn, w_ref[:, aw:2 * aw])
    ka_o[...] = ka
    kab_o[...] = ka.astype(BF16)
    va = _dot(hn, w_ref[:, 2 * aw:3 * aw])
    va_o[...] = va
    vab_o[...] = va.astype(BF16)
    qi_o[...] = _dot(hn, w_ref[:, 3 * aw:4 * aw]).astype(BF16)
    c0 = 4 * aw
    cq = _dot(hn, w_ref[:, c0:c0 + B_Q_LORA])
    ckv = _dot(hn, w_ref[:, c0 + B_Q_LORA:c0 + B_Q_LORA + B_KV_LORA])
    c1 = c0 + B_Q_LORA + B_KV_LORA
    ki2 = _dot(hn, w_ref[:, c1:c1 + 2 * IDX_DIM])
    krr = _dot(hn, w_ref[:, c1 + LANE:c1 + LANE + 256])
    wi = _dot(hn, w_ref[:, c1 + LANE + 256:c1 + 2 * LANE + 256])
    ki_o[...] = ki2[:, 0:IDX_DIM]
    ki2_o[...] = ki2.astype(BF16)
    wi_o[...] = wi[:, 0:IDX_HEADS] * IDX_SCALE
    cos = cos_ref[...]
    sin = sin_ref[...]
    krr = _rope_apply(krr, cos, sin)
    kr_o[...] = krr[:, 0:B_ROPE]
    krr_o[...] = krr.astype(BF16)
    ckvn = _rms(ckv, gkv_ref[...])
    ckv_o[...] = ckvn
    ckvb_o[...] = ckvn.astype(BF16)
    cqn = _rms(cq, gq_ref[...]).astype(BF16)
    qn = _dot(cqn, wn_ref[...]).astype(BF16)
    qr = _dot(cqn, wr_ref[...])
    qr_o[...] = _rope_apply(qr, cos, sin).astype(BF16)
    for p in range(B_HEADS // 2):
        ql_o[:, p * 512:(p + 1) * 512] = _dot(qn[:, p * LANE:(p + 1) * LANE], wk_ref[p]).astype(BF16)


def _even_proj(x, g, wts, cos, sin):
    n, d = x.shape
    tm = min(ROW_TILE, n)
    nblk = n // tm
    tblk = cos.shape[0] // tm
    row = lambda w: pl.BlockSpec((tm, w), lambda i: (i, 0))
    tab = pl.BlockSpec((tm, 256), lambda i: (i % tblk, 0))
    outs = [(A_WIDTH, BF16), (A_WIDTH, F32), (A_WIDTH, BF16), (A_WIDTH, F32), (A_WIDTH, BF16),
            (IDX_WIDTH, BF16), (IDX_DIM, F32), (2 * IDX_DIM, BF16), (IDX_HEADS, F32),
            (B_HEADS * B_KV_LORA, BF16), (B_HEADS * B_ROPE, BF16),
            (B_KV_LORA, F32), (B_KV_LORA, BF16), (B_ROPE, F32), (B_HEADS * B_ROPE, BF16)]
    res = pl.pallas_call(
        _even_proj_kernel,
        grid=(nblk,),
        in_specs=[row(d), _const_spec((1, d)), _const_spec(wts["w_in"].shape),
                  _const_spec((1, B_Q_LORA)), _const_spec(wts["w_qn"].shape), _const_spec(wts["w_qr"].shape),
                  _const_spec(wts["w_bukbd"].shape), _const_spec((1, B_KV_LORA)), tab, tab],
        out_specs=[row(w) for w, _ in outs],
        out_shape=[jax.ShapeDtypeStruct((n, w), dt) for w, dt in outs],
        compiler_params=_params(("parallel",)),
        name="even_proj",
    )(x, g, wts["w_in"], wts["g_bq"], wts["w_qn"], wts["w_qr"], wts["w_bukbd"], wts["g_bkv"], cos, sin)
    names = ["qa", "ka", "ka_bf", "va", "va_bf", "qi", "ki", "ki2", "wi", "q_lat", "q_rope",
             "ckv", "ckv_bf", "kr", "kr_rep"]
    return dict(zip(names, res))


def _sortable_key(x):
    x = jnp.where(x == 0.0, 0.0, x)
    bits = pltpu.bitcast(x, jnp.int32)
    return jnp.where(bits < 0, bits ^ jnp.int32(0x7FFFFFFF), bits)


def _topk_mask(keybuf, maskb, cbuf, npairs, n_sel, idx_bits):
    rows = keybuf.shape[1]
    lane1 = lax.broadcasted_iota(jnp.int32, (rows, LANE), 1)

    def count(pred):
        def body(c, cnt):
            a = jnp.where(pred(keybuf[2 * c], 2 * c), 1.0, 0.0)
            b = jnp.where(pred(keybuf[2 * c + 1], 2 * c + 1), 1.0, 0.0)
            return cnt + (a + b)
        cnt = lax.fori_loop(0, npairs, body, jnp.zeros((rows, LANE), F32))
        return jnp.sum(cnt, axis=1, keepdims=True)

    def bit_step(b, t_u):
        cand_u = t_u | (jnp.int32(1) << (31 - b))
        cbuf[...] = jnp.broadcast_to(cand_u ^ INT_MIN, (rows, LANE))
        tot = count(lambda kk, c: kk >= cbuf[...])
        return jnp.where(tot >= n_sel, cand_u, t_u)

    t_u = lax.fori_loop(0, 32, bit_step, jnp.zeros((rows, 1), jnp.int32))
    t_s = t_u ^ INT_MIN
    tb = jnp.broadcast_to(t_s, (rows, LANE))
    cbuf[...] = tb
    cge = count(lambda kk, c: kk >= cbuf[...])
    tie_rows = jnp.logical_and(t_s > KEY_NEG, cge > n_sel)
    any_tie = jnp.max(jnp.where(tie_rows, 1.0, 0.0)) > 0.0

    @pl.when(jnp.logical_not(any_tie))
    def _():
        def body(c, carry):
            kk = keybuf[c]
            maskb[c] = jnp.where(jnp.logical_and(kk >= cbuf[...], kk > KEY_NEG), 0.0, NEG_INF)
            return carry
        lax.fori_loop(0, 2 * npairs, body, 0)

    @pl.when(any_tie)
    def _():
        need = n_sel - count(lambda kk, c: kk > tb)

        def jstep(b, j_u):
            cand = j_u | (jnp.int32(1) << (idx_bits - 1 - b))
            cb = jnp.broadcast_to(cand, (rows, LANE))
            f = count(lambda kk, c: jnp.logical_and(kk == tb, (c * LANE + lane1) < cb))
            return jnp.where(f < need, cand, j_u)

        j_u = lax.fori_loop(0, idx_bits, jstep, jnp.zeros((rows, 1), jnp.int32))
        jb = jnp.broadcast_to(j_u, (rows, LANE))

        def body(c, carry):
            kk = keybuf[c]
            keep = jnp.logical_or(kk > tb, jnp.logical_and(kk == tb, (c * LANE + lane1) <= jb))
            maskb[c] = jnp.where(jnp.logical_and(keep, kk > KEY_NEG), 0.0, NEG_INF)
            return carry
        lax.fori_loop(0, 2 * npairs, body, 0)


def _attn_a_prompt_kernel(qi_ref, wi_ref, qa_ref, ki2_ref, ka_ref, va_ref, bias_ref, o_ref,
                          qst, wb, keybuf, maskb, cbuf, qstk, mrun, ssum, acc, pbuf, *, n_sel, idx_bits):
    i = pl.program_id(1)
    nk2 = (i + 2) // 2
    lane1 = lax.broadcasted_iota(jnp.int32, (QB, LANE), 1)
    for h in range(IDX_HEADS):
        blk = qi_ref[0, :, (h // 2) * LANE:(h // 2 + 1) * LANE]
        qst[h * QB:(h + 1) * QB, :] = jnp.where((lane1 // IDX_DIM) == (h % 2), blk, jnp.zeros_like(blk))
        wb[h] = jnp.broadcast_to(wi_ref[0, :, h:h + 1], (QB, 2 * LANE))
    row2 = lax.broadcasted_iota(jnp.int32, (QB, 2 * LANE), 0)
    col2 = lax.broadcasted_iota(jnp.int32, (QB, 2 * LANE), 1)

    def score_chunk(c, carry):
        kc = ki2_ref[0, pl.ds(pl.multiple_of(c * 256, 256), 256), :]
        d = _dot_nt(qst[...], kc)
        acc_s = wb[0] * jnp.maximum(d[0:QB, :], 0.0)
        for h in range(1, IDX_HEADS):
            acc_s = acc_s + wb[h] * jnp.maximum(d[h * QB:(h + 1) * QB, :], 0.0)
        key = _sortable_key(acc_s)
        valid = (c * 256 + col2) <= (i * QB + row2)
        key = jnp.where(valid, key, KEY_NEG)
        keybuf[2 * c] = key[:, 0:LANE]
        keybuf[2 * c + 1] = key[:, LANE:2 * LANE]
        return carry

    lax.fori_loop(0, nk2, score_chunk, 0)
    _topk_mask(keybuf, maskb, cbuf, nk2, n_sel, idx_bits)

    lane2 = col2 // A_HEAD_DIM
    nfar = jnp.maximum(nk2 - 2, 0)
    ngrp = A_HEADS // 8
    for g in range(ngrp):
        qg = qa_ref[0, :, g * 256:(g + 1) * 256]
        for hh in range(8):
            qstk[g, hh * QB:(hh + 1) * QB, :] = jnp.where(lane2 == hh, qg, jnp.zeros_like(qg))
    mrun[...] = jnp.full(mrun.shape, NEG_INF, F32)
    ssum[...] = jnp.zeros(ssum.shape, F32)
    acc[...] = jnp.zeros(acc.shape, F32)

    def logits(c, g, hh, s_all, near):
        s = s_all[hh * QB:(hh + 1) * QB, :] * A_SCALE
        halves = []
        for hf in range(2):
            l = s[:, hf * LANE:(hf + 1) * LANE] + maskb[2 * c + hf]
            if near:
                blk = 2 * c + hf
                bh = bias_ref[g * 8 + hh]
                l = l + jnp.where(blk == i, bh[:, LANE:2 * LANE],
                                  jnp.where(blk == i - 1, bh[:, 0:LANE], 0.0))
            halves.append(l)
        return halves

    def pass1(near):
        def body(c, carry):
            rows = pl.ds(pl.multiple_of(c * 256, 256), 256)
            for g in range(ngrp):
                s_all = _dot_nt(qstk[g], ka_ref[0, rows, g * 256:(g + 1) * 256])
                for hh in range(8):
                    l0, l1 = logits(c, g, hh, s_all, near)
                    sl = slice(hh * QB, (hh + 1) * QB)
                    mrun[g, sl, :] = jnp.maximum(mrun[g, sl, :], jnp.maximum(l0, l1))
            return carry
        return body

    lax.fori_loop(0, nfar, pass1(False), 0)
    lax.fori_loop(nfar, nk2, pass1(True), 0)
    for g in range(ngrp):
        mrun[g] = jnp.broadcast_to(jnp.max(mrun[g], axis=1, keepdims=True), mrun.shape[1:])

    def pass2(near):
        def body(c, carry):
            rows = pl.ds(pl.multiple_of(c * 256, 256), 256)
            for g in range(ngrp):
                s_all = _dot_nt(qstk[g], ka_ref[0, rows, g * 256:(g + 1) * 256])
                for hh in range(8):
                    l0, l1 = logits(c, g, hh, s_all, near)
                    sl = slice(hh * QB, (hh + 1) * QB)
                    m = mrun[g, sl, :]
                    p0 = jnp.exp(l0 - m)
                    p1 = jnp.exp(l1 - m)
                    ssum[g, sl, :] = ssum[g, sl, :] + (p0 + p1)
                    pbuf[g, sl, 0:LANE] = p0.astype(BF16)
                    pbuf[g, sl, LANE:2 * LANE] = p1.astype(BF16)
                acc[g] = acc[g] + _dot(pbuf[g], va_ref[0, rows, g * 256:(g + 1) * 256])
            return carry
        return body

    lax.fori_loop(0, nfar, pass2(False), 0)
    lax.fori_loop(nfar, nk2, pass2(True), 0)
    for g in range(ngrp):
        out_g = jnp.zeros((QB, 256), F32)
        for hh in range(8):
            sl = slice(hh * QB, (hh + 1) * QB)
            den = jnp.sum(ssum[g, sl, :], axis=1, keepdims=True)
            out_g = jnp.where(lane2 == hh, acc[g, sl, :] / den, out_g)
        o_ref[0, :, g * 256:(g + 1) * 256] = out_g.astype(BF16)


def _attn_a_prompt(qi, wi, qa, ki2, ka, va, bias_win):
    b, s, _ = qa.shape
    nq = s // QB
    n_sel = min(TOPK_MAX, s // 4)
    idx_bits = int(math.log2(s)) + 1
    kern = functools.partial(_attn_a_prompt_kernel, n_sel=n_sel, idx_bits=idx_bits)
    qblk = lambda w: pl.BlockSpec((1, QB, w), lambda bi, i: (bi, i, 0))
    full = lambda w: pl.BlockSpec((1, s, w), lambda bi, i: (bi, 0, 0))
    return pl.pallas_call(
        kern,
        grid=(b, nq),
        in_specs=[qblk(IDX_WIDTH), qblk(IDX_HEADS), qblk(A_WIDTH), full(2 * IDX_DIM), full(A_WIDTH), full(A_WIDTH),
                  _const_spec(bias_win.shape)],
        out_specs=qblk(A_WIDTH),
        out_shape=jax.ShapeDtypeStruct((b, s, A_WIDTH), BF16),
        scratch_shapes=[pltpu.VMEM((IDX_HEADS * QB, LANE), BF16),
                        pltpu.VMEM((IDX_HEADS, QB, 2 * LANE), F32),
                        pltpu.VMEM((nq, QB, LANE), jnp.int32),
                        pltpu.VMEM((nq, QB, LANE), F32),
                        pltpu.VMEM((QB, LANE), jnp.int32),
                        pltpu.VMEM((A_HEADS // 8, 8 * QB, 256), BF16),
                        pltpu.VMEM((A_HEADS // 8, 8 * QB, LANE), F32),
                        pltpu.VMEM((A_HEADS // 8, 8 * QB, LANE), F32),
                        pltpu.VMEM((A_HEADS // 8, 8 * QB, 256), F32),
                        pltpu.VMEM((A_HEADS // 8, 8 * QB, 256), BF16)],
        compiler_params=_params(("parallel", "arbitrary")),
        name="attn_a_prompt",
    )(qi, wi, qa, ki2, ka, va, bias_win)


def _mla_prompt_kernel(ql_ref, qr_ref, ckv_ref, kr_ref, o_ref, qst, m_scr, l_scr, acc, pbuf, *, kb, nsub):
    i = pl.program_id(1)
    j = pl.program_id(2)
    nj = pl.num_programs(2)
    qbm = nsub * QB
    last = (i * qbm + qbm - 1) // kb

    @pl.when(j == 0)
    def _():
        lane = lax.broadcasted_iota(jnp.int32, (QB, 256), 1) // B_ROPE
        for sub in range(nsub):
            qr = qr_ref[0, sub * QB:(sub + 1) * QB, :]
            for h in range(B_HEADS):
                qst[sub, h * QB:(h + 1) * QB, 0:256] = ql_ref[0, sub * QB:(sub + 1) * QB, h * 256:(h + 1) * 256]
                qst[sub, h * QB:(h + 1) * QB, 256:512] = jnp.where(lane == h, qr, jnp.zeros_like(qr))
        m_scr[...] = jnp.full(m_scr.shape, NEG_INF, F32)
        l_scr[...] = jnp.zeros(l_scr.shape, F32)
        acc[...] = jnp.zeros(acc.shape, F32)

    def step(diag):
        ckv = ckv_ref[0]
        kr = kr_ref[0]
        for sub in range(nsub):
            s_all = _dot_nt(qst[sub, :, 0:256], ckv) + _dot_nt(qst[sub, :, 256:512], kr)
            if diag:
                qpos = i * qbm + sub * QB + lax.broadcasted_iota(jnp.int32, (QB, kb), 0)
                kpos = j * kb + lax.broadcasted_iota(jnp.int32, (QB, kb), 1)
                ok = kpos <= qpos
            for h in range(B_HEADS):
                sl = slice(h * QB, (h + 1) * QB)
                s = s_all[sl, :] * MLA_SCALE
                if diag:
                    s = jnp.where(ok, s, NEG_INF)
                m_old = m_scr[sub, sl, :]
                m_new = jnp.maximum(m_old, jnp.max(s, axis=1, keepdims=True))
                alpha = jnp.exp(m_old - m_new)
                p = jnp.exp(s - m_new)
                l_scr[sub, sl, :] = alpha * l_scr[sub, sl, :] + jnp.sum(p, axis=1, keepdims=True)
                acc[sub, sl, :] = alpha * acc[sub, sl, :]
                m_scr[sub, sl, :] = m_new
                pbuf[sub, sl, :] = p.astype(BF16)
            acc[sub] = acc[sub] + _dot(pbuf[sub], ckv)

    @pl.when(j < last)
    def _():
        step(False)

    @pl.when(j == last)
    def _():
        step(True)

    @pl.when(j == nj - 1)
    def _():
        for sub in range(nsub):
            for h in range(B_HEADS):
                sl = slice(h * QB, (h + 1) * QB)
                o_ref[0, sub * QB:(sub + 1) * QB, h * 256:(h + 1) * 256] = (
                    acc[sub, sl, :] / l_scr[sub, sl, :]).astype(BF16)


def _mla_prompt(q_lat, q_rope, ckv, kr_rep):
    b, s, _ = q_lat.shape
    kb = min(MLA_KB, s)
    nsub = 2
    qbm = nsub * QB
    nq, nk = s // qbm, s // kb
    kern = functools.partial(_mla_prompt_kernel, kb=kb, nsub=nsub)
    kidx = lambda bi, i, j: (bi, jnp.minimum(j, (i * qbm + qbm - 1) // kb), 0)
    return pl.pallas_call(
        kern,
        grid=(b, nq, nk),
        in_specs=[pl.BlockSpec((1, qbm, B_HEADS * B_KV_LORA), lambda bi, i, j: (bi, i, 0)),
                  pl.BlockSpec((1, qbm, B_HEADS * B_ROPE), lambda bi, i, j: (bi, i, 0)),
                  pl.BlockSpec((1, kb, B_KV_LORA), kidx),
                  pl.BlockSpec((1, kb, B_HEADS * B_ROPE), kidx)],
        out_specs=pl.BlockSpec((1, qbm, B_HEADS * B_KV_LORA), lambda bi, i, j: (bi, i, 0)),
        out_shape=jax.ShapeDtypeStruct((b, s, B_HEADS * B_KV_LORA), BF16),
        scratch_shapes=[pltpu.VMEM((nsub, B_HEADS * QB, 512), BF16),
                        pltpu.VMEM((nsub, B_HEADS * QB, 1), F32),
                        pltpu.VMEM((nsub, B_HEADS * QB, 1), F32),
                        pltpu.VMEM((nsub, B_HEADS * QB, B_KV_LORA), F32),
                        pltpu.VMEM((nsub, B_HEADS * QB, kb), BF16)],
        compiler_params=_params(("parallel", "parallel", "arbitrary")),
        name="mla_prompt",
    )(q_lat, q_rope, ckv, kr_rep)


def _even_merge_kernel(oa_ref, obl_ref, h_ref, wbuv_ref, wout_ref, o_ref):
    obs = [_dot(obl_ref[:, p * 512:(p + 1) * 512], wbuv_ref[p]).astype(BF16) for p in range(B_HEADS // 2)]
    ob = jnp.concatenate(obs, axis=1)
    o_ref[...] = (h_ref[...] + _dot(oa_ref[...], wout_ref[0:A_WIDTH, :])
                  + _dot(ob, wout_ref[A_WIDTH:A_WIDTH + B_WIDTH, :]))


def _even_merge(oa, obl, h, wts):
    n, d = h.shape
    tm = min(ROW_TILE, n)
    row = lambda w: pl.BlockSpec((tm, w), lambda i: (i, 0))
    return pl.pallas_call(
        _even_merge_kernel,
        grid=(n // tm,),
        in_specs=[row(A_WIDTH), row(B_HEADS * B_KV_LORA), row(d),
                  _const_spec(wts["w_buvbd"].shape), _const_spec(wts["w_out_even"].shape)],
        out_specs=row(d),
        out_shape=jax.ShapeDtypeStruct((n, d), F32),
        compiler_params=_params(("parallel",)),
        name="even_merge",
    )(oa, obl, h, wts["w_buvbd"], wts["w_out_even"])


def _tail_kernel(h_ref, p_ref, gffn_ref, gple_ref, wg_ref, wu_ref, wd_ref, wpg_ref, wpp_ref, gfin_ref, o_ref,
                 *, final, fc):
    h = h_ref[...]
    hn = _rms(h, gffn_ref[...]).astype(BF16)
    dff = wg_ref.shape[1]
    acc = jnp.zeros(h.shape, F32)
    for c in range(dff // fc):
        g = _dot(hn, wg_ref[:, c * fc:(c + 1) * fc])
        u = _dot(hn, wu_ref[:, c * fc:(c + 1) * fc])
        a = (g * jax.nn.sigmoid(g) * u).astype(BF16)
        acc = acc + _dot(a, wd_ref[c * fc:(c + 1) * fc, :])
    h2 = h + acc
    gate = jax.nn.sigmoid(_dot(_rms(h2, gple_ref[...]).astype(BF16), wpg_ref[...]))
    h3 = h2 + gate * _dot(p_ref[...].astype(BF16), wpp_ref[...])
    if final:
        h3 = _rms(h3, gfin_ref[...])
    o_ref[...] = h3


def _tail(h, p, lw, g_final, final):
    n, d = h.shape
    tm = min(ROW_TILE, n)
    row = lambda w: pl.BlockSpec((tm, w), lambda i: (i, 0))
    kern = functools.partial(_tail_kernel, final=final, fc=256)
    return pl.pallas_call(
        kern,
        grid=(n // tm,),
        in_specs=[row(d), row(p.shape[1]), _const_spec((1, d)), _const_spec((1, d)),
                  _const_spec(lw["wg"].shape), _const_spec(lw["wu"].shape), _const_spec(lw["wd"].shape),
                  _const_spec(lw["wpg"].shape), _const_spec(lw["wpp"].shape), _const_spec((1, d))],
        out_specs=row(d),
        out_shape=jax.ShapeDtypeStruct((n, d), F32),
        compiler_params=_params(("parallel",)),
        name="layer_tail",
    )(h, p, lw["g_ffn"], lw["g_ple"], lw["wg"], lw["wu"], lw["wd"], lw["wpg"], lw["wpp"], g_final)


def _odd_proj_kernel(x_ref, g_ref, w_ref, q_o, k_o, kb_o, v_o, vb_o):
    hn = _rms(x_ref[...], g_ref[...]).astype(BF16)
    q_o[...] = _dot(hn, w_ref[:, 0:C_WIDTH]).astype(BF16)
    k = _dot(hn, w_ref[:, C_WIDTH:C_WIDTH + C_KV_WIDTH])
    v = _dot(hn, w_ref[:, C_WIDTH + C_KV_WIDTH:C_WIDTH + 2 * C_KV_WIDTH])
    k_o[...] = k
    kb_o[...] = k.astype(BF16)
    v_o[...] = v
    vb_o[...] = v.astype(BF16)


def _odd_proj(x, g, w):
    n, d = x.shape
    tm = min(ROW_TILE, n)
    row = lambda wd: pl.BlockSpec((tm, wd), lambda i: (i, 0))
    outs = [(C_WIDTH, BF16), (C_KV_WIDTH, F32), (C_KV_WIDTH, BF16), (C_KV_WIDTH, F32), (C_KV_WIDTH, BF16)]
    return pl.pallas_call(
        _odd_proj_kernel,
        grid=(n // tm,),
        in_specs=[row(d), _const_spec((1, d)), _const_spec(w.shape)],
        out_specs=[row(wd) for wd, _ in outs],
        out_shape=[jax.ShapeDtypeStruct((n, wd), dt) for wd, dt in outs],
        compiler_params=_params(("parallel",)),
        name="odd_proj",
    )(x, g, w)


def _swa_prompt_kernel(sink_ref, q_ref, kp_ref, kc_ref, vp_ref, vc_ref, bias_ref, h_ref, wout_ref, o_ref):
    i = pl.program_id(1)
    kk = jnp.concatenate([kp_ref[0], kc_ref[0]], axis=0)
    vv = jnp.concatenate([vp_ref[0], vc_ref[0]], axis=0)
    r = lax.broadcasted_iota(jnp.int32, (QB, 2 * WINDOW), 0)
    jj = lax.broadcasted_iota(jnp.int32, (QB, 2 * WINDOW), 1)
    dist = WINDOW + r - jj
    mask = (dist >= 0) & (dist <= WINDOW) & ((jj >= WINDOW) | (i > 0))
    lane = jj // C_HEAD_DIM
    outs = []
    for g in range(C_GROUP):
        qg = q_ref[0, :, g * 256:(g + 1) * 256]
        og = jnp.zeros((QB, 256), F32)
        for k in range(C_KV_HEADS):
            hidx = k * C_GROUP + g
            qm = jnp.where(lane == k, qg, jnp.zeros_like(qg))
            l = _dot_nt(qm, kk) * C_SCALE + bias_ref[hidx]
            l = jnp.where(mask, l, NEG_INF)
            sk = sink_ref[hidx]
            m = jnp.maximum(jnp.max(l, axis=1, keepdims=True), sk)
            e = jnp.exp(l - m)
            den = jnp.sum(e, axis=1, keepdims=True) + jnp.exp(sk - m)
            og = jnp.where(lane == k, _dot((e / den).astype(BF16), vv), og)
        outs.append(og.astype(BF16))
    o_all = jnp.concatenate(outs, axis=1)
    o_ref[0] = h_ref[0] + _dot(o_all, wout_ref[...])


def _swa_prompt(q, k, v, bias_raw, sinks, h, w_out):
    b, s, d = h.shape
    nq = s // QB
    cur = lambda w: pl.BlockSpec((1, QB, w), lambda bi, i: (bi, i, 0))
    prev = lambda w: pl.BlockSpec((1, QB, w), lambda bi, i: (bi, jnp.maximum(i - 1, 0), 0))
    return pl.pallas_call(
        _swa_prompt_kernel,
        grid=(b, nq),
        in_specs=[pl.BlockSpec(memory_space=pltpu.SMEM),
                  cur(C_WIDTH), prev(C_KV_WIDTH), cur(C_KV_WIDTH), prev(C_KV_WIDTH), cur(C_KV_WIDTH),
                  _const_spec(bias_raw.shape), cur(d), _const_spec(w_out.shape)],
        out_specs=cur(d),
        out_shape=jax.ShapeDtypeStruct((b, s, d), F32),
        compiler_params=_params(("parallel", "parallel")),
        name="swa_prompt",
    )(sinks, q, k, k, v, v, bias_raw, h, w_out)


def _rel_bucket(dist):
    n = jnp.maximum(dist, 0)
    max_exact = REL_BUCKETS // 2
    nf = jnp.maximum(n, 1).astype(F32)
    large = max_exact + (jnp.log(nf / max_exact) / math.log(REL_MAX_DIST / max_exact)
                         * (REL_BUCKETS - max_exact)).astype(jnp.int32)
    large = jnp.minimum(large, REL_BUCKETS - 1)
    return jnp.where(n < max_exact, n, large)


def _bias_window(rel_table):
    r = jnp.arange(QB)[:, None]
    j = jnp.arange(2 * QB)[None, :]
    return jnp.transpose(rel_table[_rel_bucket(WINDOW + r - j)].astype(F32), (2, 0, 1))


def _rope_tables(pos, reps):
    half = B_ROPE // 2
    inv = ROPE_THETA ** (-jnp.arange(half, dtype=F32) / half)
    ang = pos.astype(F32)[:, None] * inv
    cos, sin = jnp.cos(ang), jnp.sin(ang)
    cos_f = jnp.tile(jnp.concatenate([cos, cos], axis=1), (1, reps))
    sin_f = jnp.tile(jnp.concatenate([-sin, sin], axis=1), (1, reps))
    return cos_f, sin_f


def _block_diag_pairs(m):
    z = jnp.zeros_like(m[0])
    return jnp.stack([jnp.concatenate([jnp.concatenate([m[2 * p], z], axis=1),
                                       jnp.concatenate([z, m[2 * p + 1]], axis=1)], axis=0)
                      for p in range(m.shape[0] // 2)])


def _prep_even_weights(w_in, w_out, g_bq, w_buq, g_bkv, w_buk, w_buv):
    d = w_in.shape[0]
    splits = np.cumsum([A_WIDTH, A_WIDTH, A_WIDTH, IDX_WIDTH, IDX_DIM, IDX_HEADS, B_Q_LORA, B_KV_LORA])
    qa, ka, va, qi, ki, wi, cq, ckv, kr = jnp.split(w_in, splits.tolist(), axis=1)
    wi_pad = jnp.concatenate([wi, jnp.zeros((d, LANE - IDX_HEADS), w_in.dtype)], axis=1)
    w_all = jnp.concatenate([qa, ka, va, qi, cq, ckv, ki, ki, jnp.tile(kr, (1, B_HEADS)), wi_pad], axis=1)
    wq = w_buq.reshape(B_Q_LORA, B_HEADS, B_NOPE + B_ROPE)
    w_qn = wq[:, :, :B_NOPE].reshape(B_Q_LORA, B_HEADS * B_NOPE)
    w_qr = wq[:, :, B_NOPE:].reshape(B_Q_LORA, B_HEADS * B_ROPE)
    w_bukt = jnp.transpose(w_buk, (1, 2, 0))
    w_buvh = jnp.transpose(w_buv, (1, 0, 2))
    return {
        "w_in": w_all.astype(BF16),
        "g_bq": g_bq.reshape(1, -1), "g_bkv": g_bkv.reshape(1, -1),
        "w_qn": w_qn.astype(BF16), "w_qr": w_qr.astype(BF16),
        "w_bukbd": _block_diag_pairs(w_bukt).astype(BF16),
        "w_buvbd": _block_diag_pairs(w_buvh).astype(BF16),
        "w_out_even": w_out.astype(BF16),
    }


def _prep_odd_weights(w_in, w_out):
    d = w_in.shape[0]
    q = w_in[:, :C_WIDTH].reshape(d, C_KV_HEADS, C_GROUP, C_HEAD_DIM)
    q = jnp.transpose(q, (0, 2, 1, 3)).reshape(d, C_WIDTH)
    w_in_p = jnp.concatenate([q, w_in[:, C_WIDTH:]], axis=1).astype(BF16)
    wo = w_out.reshape(C_KV_HEADS, C_GROUP, C_HEAD_DIM, -1)
    wo = jnp.transpose(wo, (1, 0, 2, 3)).reshape(C_WIDTH, -1).astype(BF16)
    return w_in_p, wo


QP = 8
GB = LANE // QP


def _pages_per_step(n_pages, want):
    return want if n_pages % want == 0 else n_pages


def _page_index(b, j, pt_ref, *, p, pps):
    return (0, pt_ref[b, j * pps + p], 0, 0)


def _page_specs(rows, width, pps):
    return [pl.BlockSpec((None, None, rows, width), functools.partial(_page_index, p=p, pps=pps))
            for p in range(pps)]


def _slot_minor(pool):
    nd = pool.ndim
    t = jnp.transpose(pool, (0, 1) + tuple(range(3, nd)) + (2,))
    return t.reshape(pool.shape[0], pool.shape[1], -1, pool.shape[2])


def _new_key_ok(rows, bloc, t):
    row = lax.broadcasted_iota(jnp.int32, (rows, LANE), 0)
    col = lax.broadcasted_iota(jnp.int32, (rows, LANE), 1)
    return (col // QP == bloc) & (col % QP < t) & (col % QP <= row % QP)


def _online_update(s, pv, m_scr, l_scr, acc):
    m_old = m_scr[...]
    m_new = jnp.maximum(m_old, jnp.max(s, axis=1, keepdims=True))
    m_safe = jnp.where(m_new == NEG_INF, 0.0, m_new)
    alpha = jnp.exp(m_old - m_safe)
    p = jnp.exp(s - m_safe)
    l_scr[...] = alpha * l_scr[...] + jnp.sum(p, axis=1, keepdims=True)
    acc[...] = alpha * acc[...] + pv(p.astype(BF16))
    m_scr[...] = m_new


def _idx_scores_kernel(pt_ref, q_ref, w_ref, *rest, pps):
    pages, o_ref, kcat = rest[:pps], rest[pps], rest[pps + 1]
    for p in range(pps):
        kcat[:, p * PAGE:(p + 1) * PAGE] = pages[p][...].astype(BF16)
    d = _dot(q_ref[0], kcat[...])
    for p in range(pps):
        cols = slice(p * PAGE, (p + 1) * PAGE)
        a = w_ref[0, 0] * jnp.maximum(d[0:QP, cols], 0.0)
        for h in range(1, IDX_HEADS):
            a = a + w_ref[0, h] * jnp.maximum(d[h * QP:(h + 1) * QP, cols], 0.0)
        o_ref[0, :, cols] = a


def _idx_scores_sample(page_table, qi_s, w_s, idx_t):
    bd, n_pages = page_table.shape
    pps = _pages_per_step(n_pages, 32)
    kern = functools.partial(_idx_scores_kernel, pps=pps)
    grid_spec = pltpu.PrefetchScalarGridSpec(
        num_scalar_prefetch=1, grid=(bd, n_pages // pps),
        in_specs=[pl.BlockSpec((1, IDX_HEADS * QP, IDX_DIM), lambda b, j, pt: (b, 0, 0)),
                  pl.BlockSpec((1, IDX_HEADS, QP, LANE), lambda b, j, pt: (b, 0, 0, 0))]
                 + _page_specs(IDX_DIM, PAGE, pps),
        out_specs=pl.BlockSpec((1, QP, pps * PAGE), lambda b, j, pt: (b, 0, j)),
        scratch_shapes=[pltpu.VMEM((IDX_DIM, pps * PAGE), BF16)])
    return pl.pallas_call(
        kern, grid_spec=grid_spec,
        out_shape=jax.ShapeDtypeStruct((bd, QP, n_pages * PAGE), F32),
        compiler_params=_params(("parallel", "arbitrary")),
        name="idx_scores_sample",
    )(page_table, qi_s, w_s, *([idx_t] * pps))


def _select_sample_kernel(sc_ref, qi_ref, wi_ref, kin_ref, mbp_ref, mbn_ref, keybuf, maskb, cbuf,
                          *, n_sel, idx_bits, npast, t):
    rows = sc_ref.shape[0]
    row = lax.broadcasted_iota(jnp.int32, (rows, LANE), 0)
    lane1 = lax.broadcasted_iota(jnp.int32, (rows, LANE), 1)
    real = (row % QP) < t
    for c in range(npast):
        keybuf[c] = jnp.where(real, _sortable_key(sc_ref[:, c * LANE:(c + 1) * LANE]), KEY_NEG)
    a = jnp.zeros((rows, LANE), F32)
    for h in range(IDX_HEADS):
        blk = qi_ref[:, (h // 2) * LANE:(h // 2 + 1) * LANE]
        qm = jnp.where((lane1 // IDX_DIM) == (h % 2), blk, jnp.zeros_like(blk))
        a = a + wi_ref[:, h:h + 1] * jnp.maximum(_dot_nt(qm, kin_ref[...]), 0.0)
    ok = jnp.logical_and(_new_key_ok(rows, row // QP, t), real)
    keybuf[npast] = jnp.where(ok, _sortable_key(a), KEY_NEG)
    keybuf[npast + 1] = jnp.full((rows, LANE), KEY_NEG, jnp.int32)
    _topk_mask(keybuf, maskb, cbuf, (npast + 2) // 2, n_sel, idx_bits)
    for c in range(npast):
        mbp_ref[:, c * LANE:(c + 1) * LANE] = maskb[c]
    mbn_ref[...] = maskb[npast]


def _select_sample(scores, qi8, wi8, kin8, t):
    rows, past = scores.shape
    npast = past // LANE
    assert npast % 2 == 0
    n_sel = min(TOPK_MAX, (past + t) // 4)
    idx_bits = int(math.ceil(math.log2((npast + 2) * LANE))) + 1
    kern = functools.partial(_select_sample_kernel, n_sel=n_sel, idx_bits=idx_bits, npast=npast, t=t)
    rb = lambda w: pl.BlockSpec((LANE, w), lambda i: (i, 0))
    return pl.pallas_call(
        kern, grid=(rows // LANE,),
        in_specs=[rb(past), rb(IDX_WIDTH), rb(IDX_HEADS), rb(2 * IDX_DIM)],
        out_specs=[rb(past), rb(LANE)],
        out_shape=[jax.ShapeDtypeStruct((rows, past), F32), jax.ShapeDtypeStruct((rows, LANE), F32)],
        scratch_shapes=[pltpu.VMEM((npast + 2, LANE, LANE), jnp.int32),
                        pltpu.VMEM((npast + 2, LANE, LANE), F32),
                        pltpu.VMEM((LANE, LANE), jnp.int32)],
        compiler_params=_params(("parallel",)),
        name="select_sample",
    )(scores, qi8, wi8, kin8)


def _attn_a_sample_kernel(pt_ref, q_ref, mb_ref, bias_ref, mbn_ref, biasn_ref, kn_ref, vn_ref, *rest, pps):
    kpages, vpages, o_ref = rest[:pps], rest[pps:2 * pps], rest[2 * pps]
    kcat, vcat, m_scr, l_scr, acc = rest[2 * pps + 1:]
    j = pl.program_id(1)
    rows = A_HEADS * QP

    @pl.when(j == 0)
    def _():
        m_scr[...] = jnp.full(m_scr.shape, NEG_INF, F32)
        l_scr[...] = jnp.zeros(l_scr.shape, F32)
        acc[...] = jnp.zeros(acc.shape, F32)

    for p in range(pps):
        kcat[:, p * PAGE:(p + 1) * PAGE] = kpages[p][...].astype(BF16)
        vcat[:, p * PAGE:(p + 1) * PAGE] = vpages[p][...].astype(BF16)
    n = pps * PAGE
    q = q_ref[0]
    s = _dot(q, kcat[...]) * A_SCALE + bias_ref[0]
    s = (s.reshape(A_HEADS, QP, n) + mb_ref[0][None]).reshape(rows, n)
    _online_update(s, lambda p: _dot_nt(p, vcat[...]), m_scr, l_scr, acc)

    @pl.when(j == pl.num_programs(1) - 1)
    def _():
        sn = _dot_nt(q, kn_ref[...]) * A_SCALE + biasn_ref[...]
        sn = (sn.reshape(A_HEADS, QP, LANE) + mbn_ref[...][None]).reshape(rows, LANE)
        _online_update(sn, lambda p: _dot(p, vn_ref[...]), m_scr, l_scr, acc)
        l = l_scr[...]
        o = acc[...] / jnp.where(l == 0.0, 1.0, l)
        lane = lax.broadcasted_iota(jnp.int32, (QP, A_WIDTH), 1) // A_HEAD_DIM
        out = jnp.zeros((QP, A_WIDTH), F32)
        for h in range(A_HEADS):
            out = jnp.where(lane == h, o[h * QP:(h + 1) * QP, :], out)
        o_ref[0] = out.astype(BF16)


def _attn_a_sample(page_table, qa_s, mb_past, bias_pages, mb_new, bias_new, kn8, vn8, cache_k, cache_v, pps):
    bd, n_pages = page_table.shape
    n = pps * PAGE
    rows = A_HEADS * QP
    kern = functools.partial(_attn_a_sample_kernel, pps=pps)
    grid_spec = pltpu.PrefetchScalarGridSpec(
        num_scalar_prefetch=1, grid=(bd, n_pages // pps),
        in_specs=[pl.BlockSpec((1, rows, A_WIDTH), lambda b, j, pt: (b, 0, 0)),
                  pl.BlockSpec((1, QP, n), lambda b, j, pt: (b, 0, j)),
                  pl.BlockSpec((1, rows, n), lambda b, j, pt: (j, 0, 0)),
                  pl.BlockSpec((QP, LANE), lambda b, j, pt: (b, 0)),
                  pl.BlockSpec((rows, LANE), lambda b, j, pt: (0, 0)),
                  pl.BlockSpec((LANE, A_WIDTH), lambda b, j, pt: (b // GB, 0)),
                  pl.BlockSpec((LANE, A_WIDTH), lambda b, j, pt: (b // GB, 0))]
                 + _page_specs(A_WIDTH, PAGE, pps) + _page_specs(A_WIDTH, PAGE, pps),
        out_specs=pl.BlockSpec((1, QP, A_WIDTH), lambda b, j, pt: (b, 0, 0)),
        scratch_shapes=[pltpu.VMEM((A_WIDTH, n), BF16), pltpu.VMEM((A_WIDTH, n), BF16),
                        pltpu.VMEM((rows, 1), F32), pltpu.VMEM((rows, 1), F32), pltpu.VMEM((rows, A_WIDTH), F32)])
    return pl.pallas_call(
        kern, grid_spec=grid_spec,
        out_shape=jax.ShapeDtypeStruct((bd, QP, A_WIDTH), BF16),
        compiler_params=_params(("parallel", "arbitrary")),
        name="attn_a_sample",
    )(page_table, qa_s, mb_past, bias_pages, mb_new, bias_new, kn8, vn8, *([cache_k] * pps), *([cache_v] * pps))


def _mla_sample_kernel(pt_ref, ql_ref, qr_ref, cn_ref, rn_ref, *rest, pps, t):
    cpages, rpages, o_ref = rest[:pps], rest[pps:2 * pps], rest[2 * pps]
    ccat, rcat, m_scr, l_scr, acc = rest[2 * pps + 1:]
    b = pl.program_id(0)
    j = pl.program_id(1)
    rows = B_HEADS * QP

    @pl.when(j == 0)
    def _():
        m_scr[...] = jnp.full(m_scr.shape, NEG_INF, F32)
        l_scr[...] = jnp.zeros(l_scr.shape, F32)
        acc[...] = jnp.zeros(acc.shape, F32)

    for p in range(pps):
        ccat[p * PAGE:(p + 1) * PAGE, :] = cpages[p][...].astype(BF16)
        rcat[:, p * PAGE:(p + 1) * PAGE] = rpages[p][...].astype(BF16)
    ql, qr = ql_ref[0], qr_ref[0]
    s = (_dot_nt(ql, ccat[...]) + _dot(qr, rcat[...])) * MLA_SCALE
    _online_update(s, lambda p: _dot(p, ccat[...]), m_scr, l_scr, acc)

    @pl.when(j == pl.num_programs(1) - 1)
    def _():
        cn = cn_ref[...]
        sn = (_dot_nt(ql, cn) + _dot_nt(qr, rn_ref[...])) * MLA_SCALE
        sn = jnp.where(_new_key_ok(rows, b % GB, t), sn, NEG_INF)
        _online_update(sn, lambda p: _dot(p, cn), m_scr, l_scr, acc)
        o_ref[0] = (acc[...] / l_scr[...]).astype(BF16)


def _mla_sample(page_table, ql_s, qr_s, cn8, rn8, cache_ckv, kr_t, t):
    bd, n_pages = page_table.shape
    pps = _pages_per_step(n_pages, 16)
    n = pps * PAGE
    rows = B_HEADS * QP
    kern = functools.partial(_mla_sample_kernel, pps=pps, t=t)
    grid_spec = pltpu.PrefetchScalarGridSpec(
        num_scalar_prefetch=1, grid=(bd, n_pages // pps),
        in_specs=[pl.BlockSpec((1, rows, B_KV_LORA), lambda b, j, pt: (b, 0, 0)),
                  pl.BlockSpec((1, rows, B_ROPE), lambda b, j, pt: (b, 0, 0)),
                  pl.BlockSpec((LANE, B_KV_LORA), lambda b, j, pt: (b // GB, 0)),
                  pl.BlockSpec((LANE, B_ROPE), lambda b, j, pt: (b // GB, 0))]
                 + _page_specs(PAGE, B_KV_LORA, pps) + _page_specs(B_ROPE, PAGE, pps),
        out_specs=pl.BlockSpec((1, rows, B_KV_LORA), lambda b, j, pt: (b, 0, 0)),
        scratch_shapes=[pltpu.VMEM((n, B_KV_LORA), BF16), pltpu.VMEM((B_ROPE, n), BF16),
                        pltpu.VMEM((rows, 1), F32), pltpu.VMEM((rows, 1), F32), pltpu.VMEM((rows, B_KV_LORA), F32)])
    return pl.pallas_call(
        kern, grid_spec=grid_spec,
        out_shape=jax.ShapeDtypeStruct((bd, rows, B_KV_LORA), BF16),
        compiler_params=_params(("parallel", "arbitrary")),
        name="mla_sample",
    )(page_table, ql_s, qr_s, cn8, rn8, *([cache_ckv] * pps), *([kr_t] * pps))


def _swa_sample_kernel(q_ref, bk_ref, bv_ref, kn_ref, vn_ref, ks_ref, vs_ref, bb_ref, bn_ref, sk_ref,
                       o_ref, nk_ref, nv_ref, *, t):
    rows = C_HEADS * QP
    row = lax.broadcasted_iota(jnp.int32, (rows, LANE), 0)
    col = lax.broadcasted_iota(jnp.int32, (rows, LANE), 1)
    buf_ok = col >= row % QP
    lane = lax.broadcasted_iota(jnp.int32, (QP, C_KV_WIDTH), 1) // C_HEAD_DIM
    row8 = lax.broadcasted_iota(jnp.int32, (QP, C_KV_WIDTH), 0)
    sk = sk_ref[:, 0:1]
    kn, vn = kn_ref[...], vn_ref[...]
    for bi in range(GB):
        q = q_ref[bi]
        bk, bv = bk_ref[bi], bv_ref[bi]
        lb = jnp.where(buf_ok, _dot_nt(q, bk.astype(BF16)) * C_SCALE + bb_ref[...], NEG_INF)
        ln = jnp.where(_new_key_ok(rows, bi, t), _dot_nt(q, kn) * C_SCALE + bn_ref[...], NEG_INF)
        m = jnp.maximum(jnp.maximum(jnp.max(lb, axis=1, keepdims=True), jnp.max(ln, axis=1, keepdims=True)), sk)
        eb = jnp.exp(lb - m)
        en = jnp.exp(ln - m)
        den = jnp.sum(eb, axis=1, keepdims=True) + jnp.sum(en, axis=1, keepdims=True) + jnp.exp(sk - m)
        o = _dot((eb / den).astype(BF16), bv.astype(BF16)) + _dot((en / den).astype(BF16), vn)
        for g in range(C_GROUP):
            og = jnp.zeros((QP, C_KV_WIDTH), F32)
            for k in range(C_KV_HEADS):
                hidx = k * C_GROUP + g
                og = jnp.where(lane == k, o[hidx * QP:(hidx + 1) * QP, :], og)
            o_ref[bi, :, g * C_KV_WIDTH:(g + 1) * C_KV_WIDTH] = og.astype(BF16)
        for src, shifted, dst in ((bk, ks_ref, nk_ref), (bv, vs_ref, nv_ref)):
            rolled = pltpu.roll(src, WINDOW - t, 0)
            dst[bi, 0:WINDOW - QP, :] = rolled[0:WINDOW - QP, :]
            dst[bi, WINDOW - QP:WINDOW, :] = jnp.where(row8 < QP - t, rolled[WINDOW - QP:WINDOW, :], shifted[bi])


def _swa_sample(q_rows, buf_k, buf_v, kn8, vn8, k_shift, v_shift, bias_buf, bias_new, sink_b, t):
    bd = q_rows.shape[0]
    rows = C_HEADS * QP
    kern = functools.partial(_swa_sample_kernel, t=t)
    g3 = lambda r, w: pl.BlockSpec((GB, r, w), lambda i: (i, 0, 0))
    g2 = lambda w: pl.BlockSpec((LANE, w), lambda i: (i, 0))
    return pl.pallas_call(
        kern, grid=(bd // GB,),
        in_specs=[g3(rows, C_KV_WIDTH), g3(WINDOW, C_KV_WIDTH), g3(WINDOW, C_KV_WIDTH), g2(C_KV_WIDTH), g2(C_KV_WIDTH),
                  g3(QP, C_KV_WIDTH), g3(QP, C_KV_WIDTH),
                  _const_spec((rows, LANE)), _const_spec((rows, LANE)), _const_spec((rows, LANE))],
        out_specs=[g3(QP, C_WIDTH), g3(WINDOW, C_KV_WIDTH), g3(WINDOW, C_KV_WIDTH)],
        out_shape=[jax.ShapeDtypeStruct((bd, QP, C_WIDTH), BF16),
                   jax.ShapeDtypeStruct((bd, WINDOW, C_KV_WIDTH), F32),
                   jax.ShapeDtypeStruct((bd, WINDOW, C_KV_WIDTH), F32)],
        compiler_params=_params(("parallel",)),
        name="swa_sample",
    )(q_rows, buf_k, buf_v, kn8, vn8, k_shift, v_shift, bias_buf, bias_new, sink_b)


def _pad_q(a, t):
    pad = [(0, 0), (0, QP - t)] + [(0, 0)] * (a.ndim - 2)
    return jnp.pad(a, pad)


def _sample_even_attention(se, bd, t, cache_a_k, cache_a_v, cache_a_idx, cache_b_ckv, cache_b_krope,
                           page_table, bias_rel):
    n_pages = page_table.shape[1]
    pps = _pages_per_step(n_pages, 16)
    nj = n_pages // pps
    r = lambda a: a.reshape(bd, t, -1)
    qi = r(se["qi"])
    qi_s = _pad_q(jnp.transpose(qi.reshape(bd, t, IDX_HEADS, IDX_DIM), (0, 2, 1, 3)).reshape(bd * IDX_HEADS, t, IDX_DIM), t)
    qi_s = qi_s.reshape(bd, IDX_HEADS * QP, IDX_DIM)
    wi = r(se["wi"])
    w_s = _pad_q(jnp.transpose(wi, (0, 2, 1)).reshape(bd * IDX_HEADS, t), t).reshape(bd, IDX_HEADS, QP, 1)
    w_s = jnp.broadcast_to(w_s, (bd, IDX_HEADS, QP, LANE))
    scores = _idx_scores_sample(page_table, qi_s, w_s, _slot_minor(cache_a_idx))
    qi8 = _pad_q(qi, t).reshape(bd * QP, IDX_WIDTH)
    wi8 = _pad_q(wi, t).reshape(bd * QP, IDX_HEADS)
    kin8 = _pad_q(r(se["ki2"]), t).reshape(bd * QP, 2 * IDX_DIM)
    mb_past, mb_new = _select_sample(scores.reshape(bd * QP, -1), qi8, wi8, kin8, t)
    head_of_lane = jnp.arange(A_WIDTH) // A_HEAD_DIM
    qa = _pad_q(r(se["qa"]), t)
    qa_s = jnp.where(head_of_lane[None, None, None, :] == jnp.arange(A_HEADS)[None, :, None, None],
                     qa[:, None, :, :], jnp.zeros((), qa.dtype)).reshape(bd, A_HEADS * QP, A_WIDTH)
    near = bias_rel[:, :QP, :]
    bias_pages = jnp.zeros((nj, A_HEADS * QP, pps * PAGE), F32)
    bias_pages = bias_pages.at[nj - 1, :, (pps - 1) * PAGE:].set(near[:, :, :PAGE].reshape(A_HEADS * QP, PAGE))
    bias_new = jnp.tile(near[:, :, PAGE:PAGE + QP], (1, 1, GB)).reshape(A_HEADS * QP, LANE)
    kn8 = _pad_q(r(se["ka_bf"]), t).reshape(bd * QP, A_WIDTH)
    vn8 = _pad_q(r(se["va_bf"]), t).reshape(bd * QP, A_WIDTH)
    oa = _attn_a_sample(page_table, qa_s, mb_past.reshape(bd, QP, -1), bias_pages, mb_new, bias_new, kn8, vn8,
                        _slot_minor(cache_a_k), _slot_minor(cache_a_v), pps)
    oa = oa[:, :t, :].reshape(bd * t, A_WIDTH)
    hq = lambda a, w: _pad_q(jnp.transpose(a.reshape(bd, t, B_HEADS, w), (0, 2, 1, 3)).reshape(bd * B_HEADS, t, w), t
                             ).reshape(bd, B_HEADS * QP, w)
    ql_s = hq(r(se["q_lat"]), B_KV_LORA)
    qr_s = hq(r(se["q_rope"]), B_ROPE)
    cn8 = _pad_q(r(se["ckv_bf"]), t).reshape(bd * QP, B_KV_LORA)
    rn8 = _pad_q(r(se["kr"]).astype(BF16), t).reshape(bd * QP, B_ROPE)
    ob = _mla_sample(page_table, ql_s, qr_s, cn8, rn8, cache_b_ckv, _slot_minor(cache_b_krope), t)
    ob = jnp.transpose(ob.reshape(bd, B_HEADS, QP, B_KV_LORA)[:, :, :t, :], (0, 2, 1, 3))
    return oa, ob.reshape(bd * t, B_HEADS * B_KV_LORA)


def _sample_odd_attention(q, k, kb, v, vb, buf_k, buf_v, sinks, bias_raw, bd, t):
    assert buf_k.shape[1] == WINDOW
    q5 = _pad_q(q.reshape(bd, t, C_GROUP, C_KV_HEADS, C_HEAD_DIM), t)
    q5 = jnp.transpose(q5, (0, 3, 2, 1, 4))
    kv_of_lane = jnp.arange(C_KV_WIDTH) // C_HEAD_DIM
    q_rows = jnp.where(kv_of_lane[None, None, None, None, :] == jnp.arange(C_KV_HEADS)[None, :, None, None, None],
                       jnp.tile(q5, (1, 1, 1, 1, C_KV_HEADS)), jnp.zeros((), q.dtype))
    q_rows = q_rows.reshape(bd, C_HEADS * QP, C_KV_WIDTH)
    near = bias_raw[:, :QP, :]
    bias_buf = near[:, :, :WINDOW].reshape(C_HEADS * QP, WINDOW)
    bias_new = jnp.tile(near[:, :, WINDOW:WINDOW + QP], (1, 1, GB)).reshape(C_HEADS * QP, LANE)
    sink_b = jnp.broadcast_to(jnp.repeat(sinks.astype(F32), QP)[:, None], (C_HEADS * QP, LANE))
    r = lambda a: a.reshape(bd, t, C_KV_WIDTH)
    kn8 = _pad_q(r(kb), t).reshape(bd * QP, C_KV_WIDTH)
    vn8 = _pad_q(r(vb), t).reshape(bd * QP, C_KV_WIDTH)
    shift = lambda a: jnp.pad(r(a), [(0, 0), (QP - t, 0), (0, 0)])
    o8, nk, nv = _swa_sample(q_rows, buf_k.reshape(bd, WINDOW, C_KV_WIDTH), buf_v.reshape(bd, WINDOW, C_KV_WIDTH),
                             kn8, vn8, shift(k), shift(v), bias_buf, bias_new, sink_b, t)
    o = o8[:, :t, :].reshape(bd * t, C_WIDTH)
    return o, nk.reshape(bd, WINDOW, C_KV_HEADS, C_HEAD_DIM), nv.reshape(bd, WINDOW, C_KV_HEADS, C_HEAD_DIM)


def _out_proj_kernel(o_ref, h_ref, w_ref, out_ref):
    out_ref[...] = h_ref[...] + _dot(o_ref[...], w_ref[...])


def _out_proj(o, h, w):
    n, d = h.shape
    tm = min(ROW_TILE, n)
    row = lambda wd: pl.BlockSpec((tm, wd), lambda i: (i, 0))
    return pl.pallas_call(
        _out_proj_kernel,
        grid=(n // tm,),
        in_specs=[row(o.shape[1]), row(d), _const_spec(w.shape)],
        out_specs=row(d),
        out_shape=jax.ShapeDtypeStruct((n, d), F32),
        compiler_params=_params(("parallel",)),
        name="out_proj",
    )(o, h, w)


def kernel(x_prompt, x_sample, cache_a_k, cache_a_v, cache_a_idx, cache_b_ckv, cache_b_krope, state_c_k, state_c_v, page_table, p_prompt, p_sample, rel_table, w_in_even, w_out_even, g_bq, w_buq, g_bkv, w_buk, w_buv, w_in_odd, w_out_odd, c_sinks, g_mix, g_ffn, w_ffn_gate, w_ffn_up, w_ffn_down, g_ple, w_ple_gate, w_ple_proj, g_final):
    b, s, d = x_prompt.shape
    bd, t, _ = x_sample.shape
    depth = g_mix.shape[0]
    assert depth == 2 and w_in_even.shape[0] == 1 and w_in_odd.shape[0] == 1
    assert s % 256 == 0 and rel_table.shape == (REL_BUCKETS, A_HEADS)
    assert t <= QP and bd % GB == 0
    past = page_table.shape[1] * PAGE

    we = _prep_even_weights(w_in_even[0], w_out_even[0], g_bq[0], w_buq[0], g_bkv[0], w_buk[0], w_buv[0])
    w_in_o, w_out_o = _prep_odd_weights(w_in_odd[0], w_out_odd[0])
    lws = [{"g_ffn": g_ffn[i].reshape(1, d), "g_ple": g_ple[i].reshape(1, d),
            "wg": w_ffn_gate[i].astype(BF16), "wu": w_ffn_up[i].astype(BF16), "wd": w_ffn_down[i].astype(BF16),
            "wpg": w_ple_gate[i].astype(BF16), "wpp": w_ple_proj[i].astype(BF16)} for i in range(depth)]
    gfin = g_final.reshape(1, d)
    bias_raw = _bias_window(rel_table)
    bias_rel = bias_raw - rel_table[REL_BUCKETS - 1].astype(F32)[:, None, None]
    cos_p, sin_p = _rope_tables(jnp.arange(s, dtype=jnp.int32), B_HEADS)
    cos_s, sin_s = _rope_tables(jnp.tile(past + jnp.arange(t, dtype=jnp.int32), bd), B_HEADS)

    hp = x_prompt.reshape(b * s, d)
    hs = x_sample.reshape(bd * t, d)
    g0 = g_mix[0].reshape(1, d)
    g1 = g_mix[1].reshape(1, d)

    pe = _even_proj(hp, g0, we, cos_p, sin_p)
    r3 = lambda a: a.reshape(b, s, a.shape[-1])
    oa = _attn_a_prompt(r3(pe["qi"]), r3(pe["wi"]), r3(pe["qa"]), r3(pe["ki2"]), r3(pe["ka_bf"]), r3(pe["va_bf"]),
                        bias_rel)
    obl = _mla_prompt(r3(pe["q_lat"]), r3(pe["q_rope"]), r3(pe["ckv_bf"]), r3(pe["kr_rep"]))
    hp = _even_merge(oa.reshape(b * s, A_WIDTH), obl.reshape(b * s, -1), hp, we)
    hp = _tail(hp, p_prompt[0].reshape(b * s, -1), lws[0], gfin, False)

    se = _even_proj(hs, g0, we, cos_s, sin_s)
    oa_s, obl_s = _sample_even_attention(se, bd, t, cache_a_k, cache_a_v, cache_a_idx, cache_b_ckv, cache_b_krope,
                                         page_table, bias_rel)
    hs = _even_merge(oa_s, obl_s, hs, we)
    hs = _tail(hs, p_sample[0].reshape(bd * t, -1), lws[0], gfin, False)

    q, k, kb, v, vb = _odd_proj(hp, g1, w_in_o)
    hp = _swa_prompt(q.reshape(b, s, -1), kb.reshape(b, s, -1), vb.reshape(b, s, -1), bias_raw, c_sinks[0],
                     hp.reshape(b, s, d), w_out_o).reshape(b * s, d)
    y_prompt = _tail(hp, p_prompt[1].reshape(b * s, -1), lws[1], gfin, True)
    wp = min(WINDOW, s)
    pc_k = k.reshape(b, s, C_KV_HEADS, C_HEAD_DIM)[:, s - wp:]
    pc_v = v.reshape(b, s, C_KV_HEADS, C_HEAD_DIM)[:, s - wp:]

    qs, ks, ksb, vs, vsb = _odd_proj(hs, g1, w_in_o)
    os_, sc_k, sc_v = _sample_odd_attention(qs, ks, ksb, vs, vsb, state_c_k[0], state_c_v[0], c_sinks[0], bias_raw,
                                            bd, t)
    hs = _out_proj(os_, hs, w_out_o)
    y_sample = _tail(hs, p_sample[1].reshape(bd * t, -1), lws[1], gfin, True)

    hd = (A_HEADS, A_HEAD_DIM)
    return (y_prompt.reshape(b, s, d), y_sample.reshape(bd, t, d),
            pe["ka"].reshape(1, b, s, *hd), pe["va"].reshape(1, b, s, *hd), pe["ki"].reshape(1, b, s, IDX_DIM),
            pe["ckv"].reshape(1, b, s, B_KV_LORA), pe["kr"].reshape(1, b, s, B_ROPE),
            pc_k[None], pc_v[None],
            se["ka"].reshape(1, bd, t, *hd), se["va"].reshape(1, bd, t, *hd), se["ki"].reshape(1, bd, t, IDX_DIM),
            se["ckv"].reshape(1, bd, t, B_KV_LORA), se["kr"].reshape(1, bd, t, B_ROPE),
            sc_k[None], sc_v[None])
```

```python
import functools
import math

import numpy as np
import jax
import jax.numpy as jnp
from jax import lax
from jax.experimental import pallas as pl
from jax.experimental.pallas import tpu as pltpu

F32 = jnp.float32
BF16 = jnp.bfloat16

RMS_EPS = 1e-6
A_HEADS, A_HEAD_DIM = 16, 32
A_WIDTH = A_HEADS * A_HEAD_DIM
IDX_HEADS, IDX_DIM = 8, 64
IDX_WIDTH = IDX_HEADS * IDX_DIM
IDX_SCALE = IDX_WIDTH ** -0.5
TOPK_MAX = 256
B_HEADS, B_NOPE, B_ROPE, B_V = 8, 64, 32, 64
B_Q_LORA, B_KV_LORA = 256, 256
B_WIDTH = B_HEADS * B_V
MLA_SCALE = (B_NOPE + B_ROPE) ** -0.5
ROPE_THETA = 10000.0
C_HEADS, C_KV_HEADS, C_HEAD_DIM = 16, 4, 64
C_GROUP = C_HEADS // C_KV_HEADS
C_WIDTH = C_HEADS * C_HEAD_DIM
C_KV_WIDTH = C_KV_HEADS * C_HEAD_DIM
WINDOW = 128
C_SCALE = C_HEAD_DIM ** -0.5
A_SCALE = A_HEAD_DIM ** -0.5
REL_BUCKETS, REL_MAX_DIST = 32, 128
PAGE = 128
QB = 128
LANE = 128
MLA_KB = 512
ROW_TILE = 512
VMEM_LIMIT = 56 * 1024 * 1024

INT_MIN = np.int32(-2 ** 31)
KEY_NEG = np.int32(-2139095041)
NEG_INF = float("-inf")

_NT = (((1,), (1,)), ((), ()))


def _dot(a, b):
    return jnp.dot(a, b, preferred_element_type=F32)


def _dot_nt(a, b):
    return lax.dot_general(a, b, _NT, preferred_element_type=F32)


def _rms(x, g):
    ms = jnp.mean(x * x, axis=-1, keepdims=True)
    return x * lax.rsqrt(ms + RMS_EPS) * g


def _const_spec(shape):
    nd = len(shape)
    return pl.BlockSpec(shape, lambda *_: (0,) * nd, pipeline_mode=pl.Buffered(1))


def _params(sem):
    return pltpu.CompilerParams(dimension_semantics=sem, vmem_limit_bytes=VMEM_LIMIT)


def _rope_apply(x, cos, sin_signed):
    n = x.shape[1]
    lane = lax.broadcasted_iota(jnp.int32, x.shape, 1)
    partner = jnp.where((lane % B_ROPE) < (B_ROPE // 2),
                        pltpu.roll(x, n - B_ROPE // 2, 1), pltpu.roll(x, B_ROPE // 2, 1))
    return x * cos + partner * sin_signed


def _even_proj_kernel(x_ref, g_ref, w_ref, gq_ref, wn_ref, wr_ref, wk_ref, gkv_ref, cos_ref, sin_ref,
                      qa_o, ka_o, kab_o, va_o, vab_o, qi_o, ki_o, ki2_o, wi_o, ql_o, qr_o,
                      ckv_o, ckvb_o, kr_o, krr_o):
    hn = _rms(x_ref[...], g_ref[...]).astype(BF16)
    aw = A_WIDTH
    qa_o[...] = _dot(hn, w_ref[:, 0:aw]).astype(BF16)
    ka = _dot(hn, w_ref[:, aw:2 * aw])
    ka_o[...] = ka
    kab_o[...] = ka.astype(BF16)
    va = _dot(hn, w_ref[:, 2 * aw:3 * aw])
    va_o[...] = va
    vab_o[...] = va.astype(BF16)
    qi_o[...] = _dot(hn, w_ref[:, 3 * aw:4 * aw]).astype(BF16)
    c0 = 4 * aw
    cq = _dot(hn, w_ref[:, c0:c0 + B_Q_LORA])
    ckv = _dot(hn, w_ref[:, c0 + B_Q_LORA:c0 + B_Q_LORA + B_KV_LORA])
    c1 = c0 + B_Q_LORA + B_KV_LORA
    ki2 = _dot(hn, w_ref[:, c1:c1 + 2 * IDX_DIM])
    krr = _dot(hn, w_ref[:, c1 + LANE:c1 + LANE + 256])
    wi = _dot(hn, w_ref[:, c1 + LANE + 256:c1 + 2 * LANE + 256])
    ki_o[...] = ki2[:, 0:IDX_DIM]
    ki2_o[...] = ki2.astype(BF16)
    wi_o[...] = wi[:, 0:IDX_HEADS] * IDX_SCALE
    cos = cos_ref[...]
    sin = sin_ref[...]
    krr = _rope_apply(krr, cos, sin)
    kr_o[...] = krr[:, 0:B_ROPE]
    krr_o[...] = krr.astype(BF16)
    ckvn = _rms(ckv, gkv_ref[...])
    ckv_o[...] = ckvn
    ckvb_o[...] = ckvn.astype(BF16)
    cqn = _rms(cq, gq_ref[...]).astype(BF16)
    qn = _dot(cqn, wn_ref[...]).astype(BF16)
    qr = _dot(cqn, wr_ref[...])
    qr_o[...] = _rope_apply(qr, cos, sin).astype(BF16)
    for p in range(B_HEADS // 2):
        ql_o[:, p * 512:(p + 1) * 512] = _dot(qn[:, p * LANE:(p + 1) * LANE], wk_ref[p]).astype(BF16)


def _even_proj(x, g, wts, cos, sin):
    n, d = x.shape
    tm = min(ROW_TILE, n)
    nblk = n // tm
    tblk = cos.shape[0] // tm
    row = lambda w: pl.BlockSpec((tm, w), lambda i: (i, 0))
    tab = pl.BlockSpec((tm, 256), lambda i: (i % tblk, 0))
    outs = [(A_WIDTH, BF16), (A_WIDTH, F32), (A_WIDTH, BF16), (A_WIDTH, F32), (A_WIDTH, BF16),
            (IDX_WIDTH, BF16), (IDX_DIM, F32), (2 * IDX_DIM, BF16), (IDX_HEADS, F32),
            (B_HEADS * B_KV_LORA, BF16), (B_HEADS * B_ROPE, BF16),
            (B_KV_LORA, F32), (B_KV_LORA, BF16), (B_ROPE, F32), (B_HEADS * B_ROPE, BF16)]
    res = pl.pallas_call(
        _even_proj_kernel,
        grid=(nblk,),
        in_specs=[row(d), _const_spec((1, d)), _const_spec(wts["w_in"].shape),
                  _const_spec((1, B_Q_LORA)), _const_spec(wts["w_qn"].shape), _const_spec(wts["w_qr"].shape),
                  _const_spec(wts["w_bukbd"].shape), _const_spec((1, B_KV_LORA)), tab, tab],
        out_specs=[row(w) for w, _ in outs],
        out_shape=[jax.ShapeDtypeStruct((n, w), dt) for w, dt in outs],
        compiler_params=_params(("parallel",)),
        name="even_proj",
    )(x, g, wts["w_in"], wts["g_bq"], wts["w_qn"], wts["w_qr"], wts["w_bukbd"], wts["g_bkv"], cos, sin)
    names = ["qa", "ka", "ka_bf", "va", "va_bf", "qi", "ki", "ki2", "wi", "q_lat", "q_rope",
             "ckv", "ckv_bf", "kr", "kr_rep"]
    return dict(zip(names, res))


def _sortable_key(x):
    x = jnp.where(x == 0.0, 0.0, x)
    bits = pltpu.bitcast(x, jnp.int32)
    return jnp.where(bits < 0, bits ^ jnp.int32(0x7FFFFFFF), bits)


def _topk_mask(keybuf, maskb, cbuf, npairs, n_sel, idx_bits):
    rows = keybuf.shape[1]
    lane1 = lax.broadcasted_iota(jnp.int32, (rows, LANE), 1)

    def count(pred):
        def body(c, cnt):
            a = jnp.where(pred(keybuf[2 * c], 2 * c), 1.0, 0.0)
            b = jnp.where(pred(keybuf[2 * c + 1], 2 * c + 1), 1.0, 0.0)
            return cnt + (a + b)
        cnt = lax.fori_loop(0, npairs, body, jnp.zeros((rows, LANE), F32))
        return jnp.sum(cnt, axis=1, keepdims=True)

    def bit_step(b, t_u):
        cand_u = t_u | (jnp.int32(1) << (31 - b))
        cbuf[...] = jnp.broadcast_to(cand_u ^ INT_MIN, (rows, LANE))
        tot = count(lambda kk, c: kk >= cbuf[...])
        return jnp.where(tot >= n_sel, cand_u, t_u)

    t_u = lax.fori_loop(0, 32, bit_step, jnp.zeros((rows, 1), jnp.int32))
    t_s = t_u ^ INT_MIN
    tb = jnp.broadcast_to(t_s, (rows, LANE))
    cbuf[...] = tb
    cge = count(lambda kk, c: kk >= cbuf[...])
    tie_rows = jnp.logical_and(t_s > KEY_NEG, cge > n_sel)
    any_tie = jnp.max(jnp.where(tie_rows, 1.0, 0.0)) > 0.0

    @pl.when(jnp.logical_not(any_tie))
    def _():
        def body(c, carry):
            kk = keybuf[c]
            maskb[c] = jnp.where(jnp.logical_and(kk >= cbuf[...], kk > KEY_NEG), 0.0, NEG_INF)
            return carry
        lax.fori_loop(0, 2 * npairs, body, 0)

    @pl.when(any_tie)
    def _():
        need = n_sel - count(lambda kk, c: kk > tb)

        def jstep(b, j_u):
            cand = j_u | (jnp.int32(1) << (idx_bits - 1 - b))
            cb = jnp.broadcast_to(cand, (rows, LANE))
            f = count(lambda kk, c: jnp.logical_and(kk == tb, (c * LANE + lane1) < cb))
            return jnp.where(f < need, cand, j_u)

        j_u = lax.fori_loop(0, idx_bits, jstep, jnp.zeros((rows, 1), jnp.int32))
        jb = jnp.broadcast_to(j_u, (rows, LANE))

        def body(c, carry):
            kk = keybuf[c]
            keep = jnp.logical_or(kk > tb, jnp.logical_and(kk == tb, (c * LANE + lane1) <= jb))
            maskb[c] = jnp.where(jnp.logical_and(keep, kk > KEY_NEG), 0.0, NEG_INF)
            return carry
        lax.fori_loop(0, 2 * npairs, body, 0)


def _attn_a_prompt_kernel(qi_ref, wi_ref, qa_ref, ki2_ref, ka_ref, va_ref, bias_ref, o_ref,
                          qst, wb, keybuf, maskb, cbuf, qstk, mrun, ssum, acc, pbuf, *, n_sel, idx_bits):
    i = pl.program_id(1)
    nk2 = (i + 2) // 2
    lane1 = lax.broadcasted_iota(jnp.int32, (QB, LANE), 1)
    for h in range(IDX_HEADS):
        blk = qi_ref[0, :, (h // 2) * LANE:(h // 2 + 1) * LANE]
        qst[h * QB:(h + 1) * QB, :] = jnp.where((lane1 // IDX_DIM) == (h % 2), blk, jnp.zeros_like(blk))
        wb[h] = jnp.broadcast_to(wi_ref[0, :, h:h + 1], (QB, 2 * LANE))
    row2 = lax.broadcasted_iota(jnp.int32, (QB, 2 * LANE), 0)
    col2 = lax.broadcasted_iota(jnp.int32, (QB, 2 * LANE), 1)

    def score_chunk(c, carry):
        kc = ki2_ref[0, pl.ds(pl.multiple_of(c * 256, 256), 256), :]
        d = _dot_nt(qst[...], kc)
        acc_s = wb[0] * jnp.maximum(d[0:QB, :], 0.0)
        for h in range(1, IDX_HEADS):
            acc_s = acc_s + wb[h] * jnp.maximum(d[h * QB:(h + 1) * QB, :], 0.0)
        key = _sortable_key(acc_s)
        valid = (c * 256 + col2) <= (i * QB + row2)
        key = jnp.where(valid, key, KEY_NEG)
        keybuf[2 * c] = key[:, 0:LANE]
        keybuf[2 * c + 1] = key[:, LANE:2 * LANE]
        return carry

    lax.fori_loop(0, nk2, score_chunk, 0)
    _topk_mask(keybuf, maskb, cbuf, nk2, n_sel, idx_bits)

    lane2 = col2 // A_HEAD_DIM
    nfar = jnp.maximum(nk2 - 2, 0)
    ngrp = A_HEADS // 8
    for g in range(ngrp):
        qg = qa_ref[0, :, g * 256:(g + 1) * 256]
        for hh in range(8):
            qstk[g, hh * QB:(hh + 1) * QB, :] = jnp.where(lane2 == hh, qg, jnp.zeros_like(qg))
    mrun[...] = jnp.full(mrun.shape, NEG_INF, F32)
    ssum[...] = jnp.zeros(ssum.shape, F32)
    acc[...] = jnp.zeros(acc.shape, F32)

    def logits(c, g, hh, s_all, near):
        s = s_all[hh * QB:(hh + 1) * QB, :] * A_SCALE
        halves = []
        for hf in range(2):
            l = s[:, hf * LANE:(hf + 1) * LANE] + maskb[2 * c + hf]
            if near:
                blk = 2 * c + hf
                bh = bias_ref[g * 8 + hh]
                l = l + jnp.where(blk == i, bh[:, LANE:2 * LANE],
                                  jnp.where(blk == i - 1, bh[:, 0:LANE], 0.0))
            halves.append(l)
        return halves

    def pass1(near):
        def body(c, carry):
            rows = pl.ds(pl.multiple_of(c * 256, 256), 256)
            for g in range(ngrp):
                s_all = _dot_nt(qstk[g], ka_ref[0, rows, g * 256:(g + 1) * 256])
                for hh in range(8):
                    l0, l1 = logits(c, g, hh, s_all, near)
                    sl = slice(hh * QB, (hh + 1) * QB)
                    mrun[g, sl, :] = jnp.maximum(mrun[g, sl, :], jnp.maximum(l0, l1))
            return carry
        return body

    lax.fori_loop(0, nfar, pass1(False), 0)
    lax.fori_loop(nfar, nk2, pass1(True), 0)
    for g in range(ngrp):
        mrun[g] = jnp.broadcast_to(jnp.max(mrun[g], axis=1, keepdims=True), mrun.shape[1:])

    def pass2(near):
        def body(c, carry):
            rows = pl.ds(pl.multiple_of(c * 256, 256), 256)
            for g in range(ngrp):
                s_all = _dot_nt(qstk[g], ka_ref[0, rows, g * 256:(g + 1) * 256])
                for hh in range(8):
                    l0, l1 = logits(c, g, hh, s_all, near)
                    sl = slice(hh * QB, (hh + 1) * QB)
                    m = mrun[g, sl, :]
                    p0 = jnp.exp(l0 - m)
                    p1 = jnp.exp(l1 - m)
                    ssum[g, sl, :] = ssum[g, sl, :] + (p0 + p1)
                    pbuf[g, sl, 0:LANE] = p0.astype(BF16)
                    pbuf[g, sl, LANE:2 * LANE] = p1.astype(BF16)
                acc[g] = acc[g] + _dot(pbuf[g], va_ref[0, rows, g * 256:(g + 1) * 256])
            return carry
        return body

    lax.fori_loop(0, nfar, pass2(False), 0)
    lax.fori_loop(nfar, nk2, pass2(True), 0)
    for g in range(ngrp):
        out_g = jnp.zeros((QB, 256), F32)
        for hh in range(8):
            sl = slice(hh * QB, (hh + 1) * QB)
            den = jnp.sum(ssum[g, sl, :], axis=1, keepdims=True)
            out_g = jnp.where(lane2 == hh, acc[g, sl, :] / den, out_g)
        o_ref[0, :, g * 256:(g + 1) * 256] = out_g.astype(BF16)


def _attn_a_prompt(qi, wi, qa, ki2, ka, va, bias_win):
    b, s, _ = qa.shape
    nq = s // QB
    n_sel = min(TOPK_MAX, s // 4)
    idx_bits = int(math.log2(s)) + 1
    kern = functools.partial(_attn_a_prompt_kernel, n_sel=n_sel, idx_bits=idx_bits)
    qblk = lambda w: pl.BlockSpec((1, QB, w), lambda bi, i: (bi, i, 0))
    full = lambda w: pl.BlockSpec((1, s, w), lambda bi, i: (bi, 0, 0))
    return pl.pallas_call(
        kern,
        grid=(b, nq),
        in_specs=[qblk(IDX_WIDTH), qblk(IDX_HEADS), qblk(A_WIDTH), full(2 * IDX_DIM), full(A_WIDTH), full(A_WIDTH),
                  _const_spec(bias_win.shape)],
        out_specs=qblk(A_WIDTH),
        out_shape=jax.ShapeDtypeStruct((b, s, A_WIDTH), BF16),
        scratch_shapes=[pltpu.VMEM((IDX_HEADS * QB, LANE), BF16),
                        pltpu.VMEM((IDX_HEADS, QB, 2 * LANE), F32),
                        pltpu.VMEM((nq, QB, LANE), jnp.int32),
                        pltpu.VMEM((nq, QB, LANE), F32),
                        pltpu.VMEM((QB, LANE), jnp.int32),
                        pltpu.VMEM((A_HEADS // 8, 8 * QB, 256), BF16),
                        pltpu.VMEM((A_HEADS // 8, 8 * QB, LANE), F32),
                        pltpu.VMEM((A_HEADS // 8, 8 * QB, LANE), F32),
                        pltpu.VMEM((A_HEADS // 8, 8 * QB, 256), F32),
                        pltpu.VMEM((A_HEADS // 8, 8 * QB, 256), BF16)],
        compiler_params=_params(("parallel", "arbitrary")),
        name="attn_a_prompt",
    )(qi, wi, qa, ki2, ka, va, bias_win)


def _mla_prompt_kernel(ql_ref, qr_ref, ckv_ref, kr_ref, o_ref, qst, m_scr, l_scr, acc, pbuf, *, kb, nsub):
    i = pl.program_id(1)
    j = pl.program_id(2)
    nj = pl.num_programs(2)
    qbm = nsub * QB
    last = (i * qbm + qbm - 1) // kb

    @pl.when(j == 0)
    def _():
        lane = lax.broadcasted_iota(jnp.int32, (QB, 256), 1) // B_ROPE
        for sub in range(nsub):
            qr = qr_ref[0, sub * QB:(sub + 1) * QB, :]
            for h in range(B_HEADS):
                qst[sub, h * QB:(h + 1) * QB, 0:256] = ql_ref[0, sub * QB:(sub + 1) * QB, h * 256:(h + 1) * 256]
                qst[sub, h * QB:(h + 1) * QB, 256:512] = jnp.where(lane == h, qr, jnp.zeros_like(qr))
        m_scr[...] = jnp.full(m_scr.shape, NEG_INF, F32)
        l_scr[...] = jnp.zeros(l_scr.shape, F32)
        acc[...] = jnp.zeros(acc.shape, F32)

    def step(diag):
        ckv = ckv_ref[0]
        kr = kr_ref[0]
        for sub in range(nsub):
            s_all = _dot_nt(qst[sub, :, 0:256], ckv) + _dot_nt(qst[sub, :, 256:512], kr)
            if diag:
                qpos = i * qbm + sub * QB + lax.broadcasted_iota(jnp.int32, (QB, kb), 0)
                kpos = j * kb + lax.broadcasted_iota(jnp.int32, (QB, kb), 1)
                ok = kpos <= qpos
            for h in range(B_HEADS):
                sl = slice(h * QB, (h + 1) * QB)
                s = s_all[sl, :] * MLA_SCALE
                if diag:
                    s = jnp.where(ok, s, NEG_INF)
                m_old = m_scr[sub, sl, :]
                m_new = jnp.maximum(m_old, jnp.max(s, axis=1, keepdims=True))
                alpha = jnp.exp(m_old - m_new)
                p = jnp.exp(s - m_new)
                l_scr[sub, sl, :] = alpha * l_scr[sub, sl, :] + jnp.sum(p, axis=1, keepdims=True)
                acc[sub, sl, :] = alpha * acc[sub, sl, :]
                m_scr[sub, sl, :] = m_new
                pbuf[sub, sl, :] = p.astype(BF16)
            acc[sub] = acc[sub] + _dot(pbuf[sub], ckv)

    @pl.when(j < last)
    def _():
        step(False)

    @pl.when(j == last)
    def _():
        step(True)

    @pl.when(j == nj - 1)
    def _():
        for sub in range(nsub):
            for h in range(B_HEADS):
                sl = slice(h * QB, (h + 1) * QB)
                o_ref[0, sub * QB:(sub + 1) * QB, h * 256:(h + 1) * 256] = (
                    acc[sub, sl, :] / l_scr[sub, sl, :]).astype(BF16)


def _mla_prompt(q_lat, q_rope, ckv, kr_rep):
    b, s, _ = q_lat.shape
    kb = min(MLA_KB, s)
    nsub = 2
    qbm = nsub * QB
    nq, nk = s // qbm, s // kb
    kern = functools.partial(_mla_prompt_kernel, kb=kb, nsub=nsub)
    kidx = lambda bi, i, j: (bi, jnp.minimum(j, (i * qbm + qbm - 1) // kb), 0)
    return pl.pallas_call(
        kern,
        grid=(b, nq, nk),
        in_specs=[pl.BlockSpec((1, qbm, B_HEADS * B_KV_LORA), lambda bi, i, j: (bi, i, 0)),
                  pl.BlockSpec((1, qbm, B_HEADS * B_ROPE), lambda bi, i, j: (bi, i, 0)),
                  pl.BlockSpec((1, kb, B_KV_LORA), kidx),
                  pl.BlockSpec((1, kb, B_HEADS * B_ROPE), kidx)],
        out_specs=pl.BlockSpec((1, qbm, B_HEADS * B_KV_LORA), lambda bi, i, j: (bi, i, 0)),
        out_shape=jax.ShapeDtypeStruct((b, s, B_HEADS * B_KV_LORA), BF16),
        scratch_shapes=[pltpu.VMEM((nsub, B_HEADS * QB, 512), BF16),
                        pltpu.VMEM((nsub, B_HEADS * QB, 1), F32),
                        pltpu.VMEM((nsub, B_HEADS * QB, 1), F32),
                        pltpu.VMEM((nsub, B_HEADS * QB, B_KV_LORA), F32),
                        pltpu.VMEM((nsub, B_HEADS * QB, kb), BF16)],
        compiler_params=_params(("parallel", "parallel", "arbitrary")),
        name="mla_prompt",
    )(q_lat, q_rope, ckv, kr_rep)


def _even_merge_kernel(oa_ref, obl_ref, h_ref, wbuv_ref, wout_ref, o_ref):
    obs = [_dot(obl_ref[:, p * 512:(p + 1) * 512], wbuv_ref[p]).astype(BF16) for p in range(B_HEADS // 2)]
    ob = jnp.concatenate(obs, axis=1)
    o_ref[...] = (h_ref[...] + _dot(oa_ref[...], wout_ref[0:A_WIDTH, :])
                  + _dot(ob, wout_ref[A_WIDTH:A_WIDTH + B_WIDTH, :]))


def _even_merge(oa, obl, h, wts):
    n, d = h.shape
    tm = min(ROW_TILE, n)
    row = lambda w: pl.BlockSpec((tm, w), lambda i: (i, 0))
    return pl.pallas_call(
        _even_merge_kernel,
        grid=(n // tm,),
        in_specs=[row(A_WIDTH), row(B_HEADS * B_KV_LORA), row(d),
                  _const_spec(wts["w_buvbd"].shape), _const_spec(wts["w_out_even"].shape)],
        out_specs=row(d),
        out_shape=jax.ShapeDtypeStruct((n, d), F32),
        compiler_params=_params(("parallel",)),
        name="even_merge",
    )(oa, obl, h, wts["w_buvbd"], wts["w_out_even"])


def _tail_kernel(h_ref, p_ref, gffn_ref, gple_ref, wg_ref, wu_ref, wd_ref, wpg_ref, wpp_ref, gfin_ref, o_ref,
                 *, final, fc):
    h = h_ref[...]
    hn = _rms(h, gffn_ref[...]).astype(BF16)
    dff = wg_ref.shape[1]
    acc = jnp.zeros(h.shape, F32)
    for c in range(dff // fc):
        g = _dot(hn, wg_ref[:, c * fc:(c + 1) * fc])
        u = _dot(hn, wu_ref[:, c * fc:(c + 1) * fc])
        a = (g * jax.nn.sigmoid(g) * u).astype(BF16)
        acc = acc + _dot(a, wd_ref[c * fc:(c + 1) * fc, :])
    h2 = h + acc
    gate = jax.nn.sigmoid(_dot(_rms(h2, gple_ref[...]).astype(BF16), wpg_ref[...]))
    h3 = h2 + gate * _dot(p_ref[...].astype(BF16), wpp_ref[...])
    if final:
        h3 = _rms(h3, gfin_ref[...])
    o_ref[...] = h3


def _tail(h, p, lw, g_final, final):
    n, d = h.shape
    tm = min(ROW_TILE, n)
    row = lambda w: pl.BlockSpec((tm, w), lambda i: (i, 0))
    kern = functools.partial(_tail_kernel, final=final, fc=256)
    return pl.pallas_call(
        kern,
        grid=(n // tm,),
        in_specs=[row(d), row(p.shape[1]), _const_spec((1, d)), _const_spec((1, d)),
                  _const_spec(lw["wg"].shape), _const_spec(lw["wu"].shape), _const_spec(lw["wd"].shape),
                  _const_spec(lw["wpg"].shape), _const_spec(lw["wpp"].shape), _const_spec((1, d))],
        out_specs=row(d),
        out_shape=jax.ShapeDtypeStruct((n, d), F32),
        compiler_params=_params(("parallel",)),
        name="layer_tail",
    )(h, p, lw["g_ffn"], lw["g_ple"], lw["wg"], lw["wu"], lw["wd"], lw["wpg"], lw["wpp"], g_final)


def _odd_proj_kernel(x_ref, g_ref, w_ref, q_o, k_o, kb_o, v_o, vb_o):
    hn = _rms(x_ref[...], g_ref[...]).astype(BF16)
    q_o[...] = _dot(hn, w_ref[:, 0:C_WIDTH]).astype(BF16)
    k = _dot(hn, w_ref[:, C_WIDTH:C_WIDTH + C_KV_WIDTH])
    v = _dot(hn, w_ref[:, C_WIDTH + C_KV_WIDTH:C_WIDTH + 2 * C_KV_WIDTH])
    k_o[...] = k
    kb_o[...] = k.astype(BF16)
    v_o[...] = v
    vb_o[...] = v.astype(BF16)


def _odd_proj(x, g, w):
    n, d = x.shape
    tm = min(ROW_TILE, n)
    row = lambda wd: pl.BlockSpec((tm, wd), lambda i: (i, 0))
    outs = [(C_WIDTH, BF16), (C_KV_WIDTH, F32), (C_KV_WIDTH, BF16), (C_KV_WIDTH, F32), (C_KV_WIDTH, BF16)]
    return pl.pallas_call(
        _odd_proj_kernel,
        grid=(n // tm,),
        in_specs=[row(d), _const_spec((1, d)), _const_spec(w.shape)],
        out_specs=[row(wd) for wd, _ in outs],
        out_shape=[jax.ShapeDtypeStruct((n, wd), dt) for wd, dt in outs],
        compiler_params=_params(("parallel",)),
        name="odd_proj",
    )(x, g, w)


def _swa_prompt_kernel(sink_ref, q_ref, kp_ref, kc_ref, vp_ref, vc_ref, bias_ref, h_ref, wout_ref, o_ref, *, nsub):
    i = pl.program_id(1)
    kall = jnp.concatenate([kp_ref[0], kc_ref[0]], axis=0)
    vall = jnp.concatenate([vp_ref[0], vc_ref[0]], axis=0)
    r = lax.broadcasted_iota(jnp.int32, (QB, 2 * WINDOW), 0)
    jj = lax.broadcasted_iota(jnp.int32, (QB, 2 * WINDOW), 1)
    dist = WINDOW + r - jj
    band = (dist >= 0) & (dist <= WINDOW)
    lane = jj // C_HEAD_DIM
    rows_out = []
    for sub in range(nsub):
        kk = kall[sub * QB:sub * QB + 2 * WINDOW, :]
        vv = vall[sub * QB:sub * QB + 2 * WINDOW, :]
        mask = band & ((jj >= WINDOW) | (i > 0)) if sub == 0 else band
        outs = []
        for g in range(C_GROUP):
            qg = q_ref[0, sub * QB:(sub + 1) * QB, g * 256:(g + 1) * 256]
            og = jnp.zeros((QB, 256), F32)
            for k in range(C_KV_HEADS):
                hidx = k * C_GROUP + g
                qm = jnp.where(lane == k, qg, jnp.zeros_like(qg))
                l = _dot_nt(qm, kk) * C_SCALE + bias_ref[hidx]
                l = jnp.where(mask, l, NEG_INF)
                sk = sink_ref[hidx]
                m = jnp.maximum(jnp.max(l, axis=1, keepdims=True), sk)
                e = jnp.exp(l - m)
                den = jnp.sum(e, axis=1, keepdims=True) + jnp.exp(sk - m)
                og = jnp.where(lane == k, _dot((e / den).astype(BF16), vv), og)
            outs.append(og.astype(BF16))
        rows_out.append(jnp.concatenate(outs, axis=1))
    o_all = jnp.concatenate(rows_out, axis=0)
    o_ref[0] = h_ref[0] + _dot(o_all, wout_ref[...])


def _swa_prompt(q, k, v, bias_raw, sinks, h, w_out):
    b, s, d = h.shape
    nsub = 2
    qbs = nsub * QB
    nq = s // qbs
    cur = lambda w: pl.BlockSpec((1, qbs, w), lambda bi, i: (bi, i, 0))
    prev = lambda w: pl.BlockSpec((1, QB, w), lambda bi, i: (bi, jnp.maximum(nsub * i - 1, 0), 0))
    return pl.pallas_call(
        functools.partial(_swa_prompt_kernel, nsub=nsub),
        grid=(b, nq),
        in_specs=[pl.BlockSpec(memory_space=pltpu.SMEM),
                  cur(C_WIDTH), prev(C_KV_WIDTH), cur(C_KV_WIDTH), prev(C_KV_WIDTH), cur(C_KV_WIDTH),
                  _const_spec(bias_raw.shape), cur(d), _const_spec(w_out.shape)],
        out_specs=cur(d),
        out_shape=jax.ShapeDtypeStruct((b, s, d), F32),
        compiler_params=_params(("parallel", "parallel")),
        name="swa_prompt",
    )(sinks, q, k, k, v, v, bias_raw, h, w_out)


def _rel_bucket(dist):
    n = jnp.maximum(dist, 0)
    max_exact = REL_BUCKETS // 2
    nf = jnp.maximum(n, 1).astype(F32)
    large = max_exact + (jnp.log(nf / max_exact) / math.log(REL_MAX_DIST / max_exact)
                         * (REL_BUCKETS - max_exact)).astype(jnp.int32)
    large = jnp.minimum(large, REL_BUCKETS - 1)
    return jnp.where(n < max_exact, n, large)


def _bias_window(rel_table):
    r = jnp.arange(QB)[:, None]
    j = jnp.arange(2 * QB)[None, :]
    bucket = _rel_bucket(WINDOW + r - j)
    onehot = (bucket[None] == jnp.arange(REL_BUCKETS)[:, None, None]).astype(F32)
    return jnp.einsum('bh,brj->hrj', rel_table.astype(F32), onehot, precision=lax.Precision.HIGHEST)


def _rope_tables(pos, reps):
    half = B_ROPE // 2
    inv = ROPE_THETA ** (-jnp.arange(half, dtype=F32) / half)
    ang = pos.astype(F32)[:, None] * inv
    cos, sin = jnp.cos(ang), jnp.sin(ang)
    cos_f = jnp.tile(jnp.concatenate([cos, cos], axis=1), (1, reps))
    sin_f = jnp.tile(jnp.concatenate([-sin, sin], axis=1), (1, reps))
    return cos_f, sin_f


def _block_diag_pairs(m):
    z = jnp.zeros_like(m[0])
    return jnp.stack([jnp.concatenate([jnp.concatenate([m[2 * p], z], axis=1),
                                       jnp.concatenate([z, m[2 * p + 1]], axis=1)], axis=0)
                      for p in range(m.shape[0] // 2)])


def _prep_even_weights(w_in, w_out, g_bq, w_buq, g_bkv, w_buk, w_buv):
    d = w_in.shape[0]
    splits = np.cumsum([A_WIDTH, A_WIDTH, A_WIDTH, IDX_WIDTH, IDX_DIM, IDX_HEADS, B_Q_LORA, B_KV_LORA])
    qa, ka, va, qi, ki, wi, cq, ckv, kr = jnp.split(w_in, splits.tolist(), axis=1)
    wi_pad = jnp.concatenate([wi, jnp.zeros((d, LANE - IDX_HEADS), w_in.dtype)], axis=1)
    w_all = jnp.concatenate([qa, ka, va, qi, cq, ckv, ki, ki, jnp.tile(kr, (1, B_HEADS)), wi_pad], axis=1)
    wq = w_buq.reshape(B_Q_LORA, B_HEADS, B_NOPE + B_ROPE)
    w_qn = wq[:, :, :B_NOPE].reshape(B_Q_LORA, B_HEADS * B_NOPE)
    w_qr = wq[:, :, B_NOPE:].reshape(B_Q_LORA, B_HEADS * B_ROPE)
    w_bukt = jnp.transpose(w_buk, (1, 2, 0))
    w_buvh = jnp.transpose(w_buv, (1, 0, 2))
    return {
        "w_in": w_all.astype(BF16),
        "g_bq": g_bq.reshape(1, -1), "g_bkv": g_bkv.reshape(1, -1),
        "w_qn": w_qn.astype(BF16), "w_qr": w_qr.astype(BF16),
        "w_bukbd": _block_diag_pairs(w_bukt).astype(BF16),
        "w_buvbd": _block_diag_pairs(w_buvh).astype(BF16),
        "w_out_even": w_out.astype(BF16),
    }


def _prep_odd_weights(w_in, w_out):
    d = w_in.shape[0]
    q = w_in[:, :C_WIDTH].reshape(d, C_KV_HEADS, C_GROUP, C_HEAD_DIM)
    q = jnp.transpose(q, (0, 2, 1, 3)).reshape(d, C_WIDTH)
    w_in_p = jnp.concatenate([q, w_in[:, C_WIDTH:]], axis=1).astype(BF16)
    wo = w_out.reshape(C_KV_HEADS, C_GROUP, C_HEAD_DIM, -1)
    wo = jnp.transpose(wo, (1, 0, 2, 3)).reshape(C_WIDTH, -1).astype(BF16)
    return w_in_p, wo


QP = 8
GB = LANE // QP


def _pages_per_step(n_pages, want):
    return want if n_pages % want == 0 else n_pages


def _page_index(b, j, pt_ref, *, p, pps):
    return (0, pt_ref[b, j * pps + p], 0, 0)


def _page_specs(rows, width, pps):
    return [pl.BlockSpec((None, None, rows, width), functools.partial(_page_index, p=p, pps=pps))
            for p in range(pps)]


def _slot_minor(pool):
    nd = pool.ndim
    t = jnp.transpose(pool, (0, 1) + tuple(range(3, nd)) + (2,))
    return t.reshape(pool.shape[0], pool.shape[1], -1, pool.shape[2])


def _new_key_ok(rows, bloc, t):
    row = lax.broadcasted_iota(jnp.int32, (rows, LANE), 0)
    col = lax.broadcasted_iota(jnp.int32, (rows, LANE), 1)
    return (col // QP == bloc) & (col % QP < t) & (col % QP <= row % QP)


def _online_update(s, pv, m_scr, l_scr, acc):
    m_old = m_scr[...]
    m_new = jnp.maximum(m_old, jnp.max(s, axis=1, keepdims=True))
    m_safe = jnp.where(m_new == NEG_INF, 0.0, m_new)
    alpha = jnp.exp(m_old - m_safe)
    p = jnp.exp(s - m_safe)
    l_scr[...] = alpha * l_scr[...] + jnp.sum(p, axis=1, keepdims=True)
    acc[...] = alpha * acc[...] + pv(p.astype(BF16))
    m_scr[...] = m_new


def _idx_scores_kernel(pt_ref, q_ref, w_ref, *rest, pps):
    pages, o_ref, kcat = rest[:pps], rest[pps], rest[pps + 1]
    for p in range(pps):
        kcat[:, p * PAGE:(p + 1) * PAGE] = pages[p][...].astype(BF16)
    d = _dot(q_ref[0], kcat[...])
    for p in range(pps):
        cols = slice(p * PAGE, (p + 1) * PAGE)
        a = w_ref[0, 0] * jnp.maximum(d[0:QP, cols], 0.0)
        for h in range(1, IDX_HEADS):
            a = a + w_ref[0, h] * jnp.maximum(d[h * QP:(h + 1) * QP, cols], 0.0)
        o_ref[0, :, cols] = a


def _idx_scores_sample(page_table, qi_s, w_s, idx_t):
    bd, n_pages = page_table.shape
    pps = _pages_per_step(n_pages, 32)
    kern = functools.partial(_idx_scores_kernel, pps=pps)
    grid_spec = pltpu.PrefetchScalarGridSpec(
        num_scalar_prefetch=1, grid=(bd, n_pages // pps),
        in_specs=[pl.BlockSpec((1, IDX_HEADS * QP, IDX_DIM), lambda b, j, pt: (b, 0, 0)),
                  pl.BlockSpec((1, IDX_HEADS, QP, LANE), lambda b, j, pt: (b, 0, 0, 0))]
                 + _page_specs(IDX_DIM, PAGE, pps),
        out_specs=pl.BlockSpec((1, QP, pps * PAGE), lambda b, j, pt: (b, 0, j)),
        scratch_shapes=[pltpu.VMEM((IDX_DIM, pps * PAGE), BF16)])
    return pl.pallas_call(
        kern, grid_spec=grid_spec,
        out_shape=jax.ShapeDtypeStruct((bd, QP, n_pages * PAGE), F32),
        compiler_params=_params(("parallel", "arbitrary")),
        name="idx_scores_sample",
    )(page_table, qi_s, w_s, *([idx_t] * pps))


def _select_sample_kernel(sc_ref, qi_ref, wi_ref, kin_ref, mbp_ref, mbn_ref, keybuf, maskb, cbuf,
                          *, n_sel, idx_bits, npast, t):
    rows = sc_ref.shape[0]
    row = lax.broadcasted_iota(jnp.int32, (rows, LANE), 0)
    lane1 = lax.broadcasted_iota(jnp.int32, (rows, LANE), 1)
    real = (row % QP) < t
    for c in range(npast):
        keybuf[c] = jnp.where(real, _sortable_key(sc_ref[:, c * LANE:(c + 1) * LANE]), KEY_NEG)
    a = jnp.zeros((rows, LANE), F32)
    for h in range(IDX_HEADS):
        blk = qi_ref[:, (h // 2) * LANE:(h // 2 + 1) * LANE]
        qm = jnp.where((lane1 // IDX_DIM) == (h % 2), blk, jnp.zeros_like(blk))
        a = a + wi_ref[:, h:h + 1] * jnp.maximum(_dot_nt(qm, kin_ref[...]), 0.0)
    ok = jnp.logical_and(_new_key_ok(rows, row // QP, t), real)
    keybuf[npast] = jnp.where(ok, _sortable_key(a), KEY_NEG)
    keybuf[npast + 1] = jnp.full((rows, LANE), KEY_NEG, jnp.int32)
    _topk_mask(keybuf, maskb, cbuf, (npast + 2) // 2, n_sel, idx_bits)
    for c in range(npast):
        mbp_ref[:, c * LANE:(c + 1) * LANE] = maskb[c]
    mbn_ref[...] = maskb[npast]


def _select_sample(scores, qi8, wi8, kin8, t):
    rows, past = scores.shape
    npast = past // LANE
    assert npast % 2 == 0
    n_sel = min(TOPK_MAX, (past + t) // 4)
    idx_bits = int(math.ceil(math.log2((npast + 2) * LANE))) + 1
    kern = functools.partial(_select_sample_kernel, n_sel=n_sel, idx_bits=idx_bits, npast=npast, t=t)
    rb = lambda w: pl.BlockSpec((LANE, w), lambda i: (i, 0))
    return pl.pallas_call(
        kern, grid=(rows // LANE,),
        in_specs=[rb(past), rb(IDX_WIDTH), rb(IDX_HEADS), rb(2 * IDX_DIM)],
        out_specs=[rb(past), rb(LANE)],
        out_shape=[jax.ShapeDtypeStruct((rows, past), F32), jax.ShapeDtypeStruct((rows, LANE), F32)],
        scratch_shapes=[pltpu.VMEM((npast + 2, LANE, LANE), jnp.int32),
                        pltpu.VMEM((npast + 2, LANE, LANE), F32),
                        pltpu.VMEM((LANE, LANE), jnp.int32)],
        compiler_params=_params(("parallel",)),
        name="select_sample",
    )(scores, qi8, wi8, kin8)


def _attn_a_sample_kernel(pt_ref, q_ref, mb_ref, bias_ref, mbn_ref, biasn_ref, kn_ref, vn_ref, *rest, pps):
    kpages, vpages, o_ref = rest[:pps], rest[pps:2 * pps], rest[2 * pps]
    kcat, vcat, m_scr, l_scr, acc = rest[2 * pps + 1:]
    j = pl.program_id(1)
    rows = A_HEADS * QP

    @pl.when(j == 0)
    def _():
        m_scr[...] = jnp.full(m_scr.shape, NEG_INF, F32)
        l_scr[...] = jnp.zeros(l_scr.shape, F32)
        acc[...] = jnp.zeros(acc.shape, F32)

    for p in range(pps):
        kcat[:, p * PAGE:(p + 1) * PAGE] = kpages[p][...].astype(BF16)
        vcat[:, p * PAGE:(p + 1) * PAGE] = vpages[p][...].astype(BF16)
    n = pps * PAGE
    q = q_ref[0]
    s = _dot(q, kcat[...]) * A_SCALE + bias_ref[0]
    s = (s.reshape(A_HEADS, QP, n) + mb_ref[0][None]).reshape(rows, n)
    _online_update(s, lambda p: _dot_nt(p, vcat[...]), m_scr, l_scr, acc)

    @pl.when(j == pl.num_programs(1) - 1)
    def _():
        sn = _dot_nt(q, kn_ref[...]) * A_SCALE + biasn_ref[...]
        sn = (sn.reshape(A_HEADS, QP, LANE) + mbn_ref[...][None]).reshape(rows, LANE)
        _online_update(sn, lambda p: _dot(p, vn_ref[...]), m_scr, l_scr, acc)
        l = l_scr[...]
        o = acc[...] / jnp.where(l == 0.0, 1.0, l)
        lane = lax.broadcasted_iota(jnp.int32, (QP, A_WIDTH), 1) // A_HEAD_DIM
        out = jnp.zeros((QP, A_WIDTH), F32)
        for h in range(A_HEADS):
            out = jnp.where(lane == h, o[h * QP:(h + 1) * QP, :], out)
        o_ref[0] = out.astype(BF16)


def _attn_a_sample(page_table, qa_s, mb_past, bias_pages, mb_new, bias_new, kn8, vn8, cache_k, cache_v, pps):
    bd, n_pages = page_table.shape
    n = pps * PAGE
    rows = A_HEADS * QP
    kern = functools.partial(_attn_a_sample_kernel, pps=pps)
    grid_spec = pltpu.PrefetchScalarGridSpec(
        num_scalar_prefetch=1, grid=(bd, n_pages // pps),
        in_specs=[pl.BlockSpec((1, rows, A_WIDTH), lambda b, j, pt: (b, 0, 0)),
                  pl.BlockSpec((1, QP, n), lambda b, j, pt: (b, 0, j)),
                  pl.BlockSpec((1, rows, n), lambda b, j, pt: (j, 0, 0)),
                  pl.BlockSpec((QP, LANE), lambda b, j, pt: (b, 0)),
                  pl.BlockSpec((rows, LANE), lambda b, j, pt: (0, 0)),
                  pl.BlockSpec((LANE, A_WIDTH), lambda b, j, pt: (b // GB, 0)),
                  pl.BlockSpec((LANE, A_WIDTH), lambda b, j, pt: (b // GB, 0))]
                 + _page_specs(A_WIDTH, PAGE, pps) + _page_specs(A_WIDTH, PAGE, pps),
        out_specs=pl.BlockSpec((1, QP, A_WIDTH), lambda b, j, pt: (b, 0, 0)),
        scratch_shapes=[pltpu.VMEM((A_WIDTH, n), BF16), pltpu.VMEM((A_WIDTH, n), BF16),
                        pltpu.VMEM((rows, 1), F32), pltpu.VMEM((rows, 1), F32), pltpu.VMEM((rows, A_WIDTH), F32)])
    return pl.pallas_call(
        kern, grid_spec=grid_spec,
        out_shape=jax.ShapeDtypeStruct((bd, QP, A_WIDTH), BF16),
        compiler_params=_params(("parallel", "arbitrary")),
        name="attn_a_sample",
    )(page_table, qa_s, mb_past, bias_pages, mb_new, bias_new, kn8, vn8, *([cache_k] * pps), *([cache_v] * pps))


def _mla_sample_kernel(pt_ref, ql_ref, qr_ref, cn_ref, rn_ref, *rest, pps, t):
    cpages, rpages, o_ref = rest[:pps], rest[pps:2 * pps], rest[2 * pps]
    ccat, rcat, m_scr, l_scr, acc = rest[2 * pps + 1:]
    b = pl.program_id(0)
    j = pl.program_id(1)
    rows = B_HEADS * QP

    @pl.when(j == 0)
    def _():
        m_scr[...] = jnp.full(m_scr.shape, NEG_INF, F32)
        l_scr[...] = jnp.zeros(l_scr.shape, F32)
        acc[...] = jnp.zeros(acc.shape, F32)

    for p in range(pps):
        ccat[p * PAGE:(p + 1) * PAGE, :] = cpages[p][...].astype(BF16)
        rcat[:, p * PAGE:(p + 1) * PAGE] = rpages[p][...].astype(BF16)
    ql, qr = ql_ref[0], qr_ref[0]
    s = (_dot_nt(ql, ccat[...]) + _dot(qr, rcat[...])) * MLA_SCALE
    _online_update(s, lambda p: _dot(p, ccat[...]), m_scr, l_scr, acc)

    @pl.when(j == pl.num_programs(1) - 1)
    def _():
        cn = cn_ref[...]
        sn = (_dot_nt(ql, cn) + _dot_nt(qr, rn_ref[...])) * MLA_SCALE
        sn = jnp.where(_new_key_ok(rows, b % GB, t), sn, NEG_INF)
        _online_update(sn, lambda p: _dot(p, cn), m_scr, l_scr, acc)
        o_ref[0] = (acc[...] / l_scr[...]).astype(BF16)


def _mla_sample(page_table, ql_s, qr_s, cn8, rn8, cache_ckv, kr_t, t):
    bd, n_pages = page_table.shape
    pps = _pages_per_step(n_pages, 16)
    n = pps * PAGE
    rows = B_HEADS * QP
    kern = functools.partial(_mla_sample_kernel, pps=pps, t=t)
    grid_spec = pltpu.PrefetchScalarGridSpec(
        num_scalar_prefetch=1, grid=(bd, n_pages // pps),
        in_specs=[pl.BlockSpec((1, rows, B_KV_LORA), lambda b, j, pt: (b, 0, 0)),
                  pl.BlockSpec((1, rows, B_ROPE), lambda b, j, pt: (b, 0, 0)),
                  pl.BlockSpec((LANE, B_KV_LORA), lambda b, j, pt: (b // GB, 0)),
                  pl.BlockSpec((LANE, B_ROPE), lambda b, j, pt: (b // GB, 0))]
                 + _page_specs(PAGE, B_KV_LORA, pps) + _page_specs(B_ROPE, PAGE, pps),
        out_specs=pl.BlockSpec((1, rows, B_KV_LORA), lambda b, j, pt: (b, 0, 0)),
        scratch_shapes=[pltpu.VMEM((n, B_KV_LORA), BF16), pltpu.VMEM((B_ROPE, n), BF16),
                        pltpu.VMEM((rows, 1), F32), pltpu.VMEM((rows, 1), F32), pltpu.VMEM((rows, B_KV_LORA), F32)])
    return pl.pallas_call(
        kern, grid_spec=grid_spec,
        out_shape=jax.ShapeDtypeStruct((bd, rows, B_KV_LORA), BF16),
        compiler_params=_params(("parallel", "arbitrary")),
        name="mla_sample",
    )(page_table, ql_s, qr_s, cn8, rn8, *([cache_ckv] * pps), *([kr_t] * pps))


def _swa_sample_kernel(q_ref, bk_ref, bv_ref, kn_ref, vn_ref, ks_ref, vs_ref, bb_ref, bn_ref, sk_ref,
                       o_ref, nk_ref, nv_ref, *, t):
    rows = C_HEADS * QP
    row = lax.broadcasted_iota(jnp.int32, (rows, LANE), 0)
    col = lax.broadcasted_iota(jnp.int32, (rows, LANE), 1)
    buf_ok = col >= row % QP
    lane = lax.broadcasted_iota(jnp.int32, (QP, C_KV_WIDTH), 1) // C_HEAD_DIM
    row8 = lax.broadcasted_iota(jnp.int32, (QP, C_KV_WIDTH), 0)
    sk = sk_ref[:, 0:1]
    kn, vn = kn_ref[...], vn_ref[...]
    for bi in range(GB):
        q = q_ref[bi]
        bk, bv = bk_ref[bi], bv_ref[bi]
        lb = jnp.where(buf_ok, _dot_nt(q, bk.astype(BF16)) * C_SCALE + bb_ref[...], NEG_INF)
        ln = jnp.where(_new_key_ok(rows, bi, t), _dot_nt(q, kn) * C_SCALE + bn_ref[...], NEG_INF)
        m = jnp.maximum(jnp.maximum(jnp.max(lb, axis=1, keepdims=True), jnp.max(ln, axis=1, keepdims=True)), sk)
        eb = jnp.exp(lb - m)
        en = jnp.exp(ln - m)
        den = jnp.sum(eb, axis=1, keepdims=True) + jnp.sum(en, axis=1, keepdims=True) + jnp.exp(sk - m)
        o = _dot((eb / den).astype(BF16), bv.astype(BF16)) + _dot((en / den).astype(BF16), vn)
        for g in range(C_GROUP):
            og = jnp.zeros((QP, C_KV_WIDTH), F32)
            for k in range(C_KV_HEADS):
                hidx = k * C_GROUP + g
                og = jnp.where(lane == k, o[hidx * QP:(hidx + 1) * QP, :], og)
            o_ref[bi, :, g * C_KV_WIDTH:(g + 1) * C_KV_WIDTH] = og.astype(BF16)
        for src, shifted, dst in ((bk, ks_ref, nk_ref), (bv, vs_ref, nv_ref)):
            rolled = pltpu.roll(src, WINDOW - t, 0)
            dst[bi, 0:WINDOW - QP, :] = rolled[0:WINDOW - QP, :]
            dst[bi, WINDOW - QP:WINDOW, :] = jnp.where(row8 < QP - t, rolled[WINDOW - QP:WINDOW, :], shifted[bi])


def _swa_sample(q_rows, buf_k, buf_v, kn8, vn8, k_shift, v_shift, bias_buf, bias_new, sink_b, t):
    bd = q_rows.shape[0]
    rows = C_HEADS * QP
    kern = functools.partial(_swa_sample_kernel, t=t)
    g3 = lambda r, w: pl.BlockSpec((GB, r, w), lambda i: (i, 0, 0))
    g2 = lambda w: pl.BlockSpec((LANE, w), lambda i: (i, 0))
    return pl.pallas_call(
        kern, grid=(bd // GB,),
        in_specs=[g3(rows, C_KV_WIDTH), g3(WINDOW, C_KV_WIDTH), g3(WINDOW, C_KV_WIDTH), g2(C_KV_WIDTH), g2(C_KV_WIDTH),
                  g3(QP, C_KV_WIDTH), g3(QP, C_KV_WIDTH),
                  _const_spec((rows, LANE)), _const_spec((rows, LANE)), _const_spec((rows, LANE))],
        out_specs=[g3(QP, C_WIDTH), g3(WINDOW, C_KV_WIDTH), g3(WINDOW, C_KV_WIDTH)],
        out_shape=[jax.ShapeDtypeStruct((bd, QP, C_WIDTH), BF16),
                   jax.ShapeDtypeStruct((bd, WINDOW, C_KV_WIDTH), F32),
                   jax.ShapeDtypeStruct((bd, WINDOW, C_KV_WIDTH), F32)],
        compiler_params=_params(("parallel",)),
        name="swa_sample",
    )(q_rows, buf_k, buf_v, kn8, vn8, k_shift, v_shift, bias_buf, bias_new, sink_b)


def _pad_q(a, t):
    pad = [(0, 0), (0, QP - t)] + [(0, 0)] * (a.ndim - 2)
    return jnp.pad(a, pad)


def _sample_even_attention(se, bd, t, cache_a_k, cache_a_v, cache_a_idx, cache_b_ckv, cache_b_krope,
                           page_table, bias_rel):
    n_pages = page_table.shape[1]
    pps = _pages_per_step(n_pages, 16)
    nj = n_pages // pps
    r = lambda a: a.reshape(bd, t, -1)
    qi = r(se["qi"])
    qi_s = _pad_q(jnp.transpose(qi.reshape(bd, t, IDX_HEADS, IDX_DIM), (0, 2, 1, 3)).reshape(bd * IDX_HEADS, t, IDX_DIM), t)
    qi_s = qi_s.reshape(bd, IDX_HEADS * QP, IDX_DIM)
    wi = r(se["wi"])
    w_s = _pad_q(jnp.transpose(wi, (0, 2, 1)).reshape(bd * IDX_HEADS, t), t).reshape(bd, IDX_HEADS, QP, 1)
    w_s = jnp.broadcast_to(w_s, (bd, IDX_HEADS, QP, LANE))
    scores = _idx_scores_sample(page_table, qi_s, w_s, _slot_minor(cache_a_idx))
    qi8 = _pad_q(qi, t).reshape(bd * QP, IDX_WIDTH)
    wi8 = _pad_q(wi, t).reshape(bd * QP, IDX_HEADS)
    kin8 = _pad_q(r(se["ki2"]), t).reshape(bd * QP, 2 * IDX_DIM)
    mb_past, mb_new = _select_sample(scores.reshape(bd * QP, -1), qi8, wi8, kin8, t)
    head_of_lane = jnp.arange(A_WIDTH) // A_HEAD_DIM
    qa = _pad_q(r(se["qa"]), t)
    qa_s = jnp.where(head_of_lane[None, None, None, :] == jnp.arange(A_HEADS)[None, :, None, None],
                     qa[:, None, :, :], jnp.zeros((), qa.dtype)).reshape(bd, A_HEADS * QP, A_WIDTH)
    near = bias_rel[:, :QP, :]
    bias_pages = jnp.zeros((nj, A_HEADS * QP, pps * PAGE), F32)
    bias_pages = bias_pages.at[nj - 1, :, (pps - 1) * PAGE:].set(near[:, :, :PAGE].reshape(A_HEADS * QP, PAGE))
    bias_new = jnp.tile(near[:, :, PAGE:PAGE + QP], (1, 1, GB)).reshape(A_HEADS * QP, LANE)
    kn8 = _pad_q(r(se["ka_bf"]), t).reshape(bd * QP, A_WIDTH)
    vn8 = _pad_q(r(se["va_bf"]), t).reshape(bd * QP, A_WIDTH)
    oa = _attn_a_sample(page_table, qa_s, mb_past.reshape(bd, QP, -1), bias_pages, mb_new, bias_new, kn8, vn8,
                        _slot_minor(cache_a_k), _slot_minor(cache_a_v), pps)
    oa = oa[:, :t, :].reshape(bd * t, A_WIDTH)
    hq = lambda a, w: _pad_q(jnp.transpose(a.reshape(bd, t, B_HEADS, w), (0, 2, 1, 3)).reshape(bd * B_HEADS, t, w), t
                             ).reshape(bd, B_HEADS * QP, w)
    ql_s = hq(r(se["q_lat"]), B_KV_LORA)
    qr_s = hq(r(se["q_rope"]), B_ROPE)
    cn8 = _pad_q(r(se["ckv_bf"]), t).reshape(bd * QP, B_KV_LORA)
    rn8 = _pad_q(r(se["kr"]).astype(BF16), t).reshape(bd * QP, B_ROPE)
    ob = _mla_sample(page_table, ql_s, qr_s, cn8, rn8, cache_b_ckv, _slot_minor(cache_b_krope), t)
    ob = jnp.transpose(ob.reshape(bd, B_HEADS, QP, B_KV_LORA)[:, :, :t, :], (0, 2, 1, 3))
    return oa, ob.reshape(bd * t, B_HEADS * B_KV_LORA)


def _sample_odd_attention(q, k, kb, v, vb, buf_k, buf_v, sinks, bias_raw, bd, t):
    assert buf_k.shape[1] == WINDOW
    q5 = _pad_q(q.reshape(bd, t, C_GROUP, C_KV_HEADS, C_HEAD_DIM), t)
    q5 = jnp.transpose(q5, (0, 3, 2, 1, 4))
    kv_of_lane = jnp.arange(C_KV_WIDTH) // C_HEAD_DIM
    q_rows = jnp.where(kv_of_lane[None, None, None, None, :] == jnp.arange(C_KV_HEADS)[None, :, None, None, None],
                       jnp.tile(q5, (1, 1, 1, 1, C_KV_HEADS)), jnp.zeros((), q.dtype))
    q_rows = q_rows.reshape(bd, C_HEADS * QP, C_KV_WIDTH)
    near = bias_raw[:, :QP, :]
    bias_buf = near[:, :, :WINDOW].reshape(C_HEADS * QP, WINDOW)
    bias_new = jnp.tile(near[:, :, WINDOW:WINDOW + QP], (1, 1, GB)).reshape(C_HEADS * QP, LANE)
    sink_b = jnp.broadcast_to(jnp.repeat(sinks.astype(F32), QP)[:, None], (C_HEADS * QP, LANE))
    r = lambda a: a.reshape(bd, t, C_KV_WIDTH)
    kn8 = _pad_q(r(kb), t).reshape(bd * QP, C_KV_WIDTH)
    vn8 = _pad_q(r(vb), t).reshape(bd * QP, C_KV_WIDTH)
    shift = lambda a: jnp.pad(r(a), [(0, 0), (QP - t, 0), (0, 0)])
    o8, nk, nv = _swa_sample(q_rows, buf_k.reshape(bd, WINDOW, C_KV_WIDTH), buf_v.reshape(bd, WINDOW, C_KV_WIDTH),
                             kn8, vn8, shift(k), shift(v), bias_buf, bias_new, sink_b, t)
    o = o8[:, :t, :].reshape(bd * t, C_WIDTH)
    return o, nk.reshape(bd, WINDOW, C_KV_HEADS, C_HEAD_DIM), nv.reshape(bd, WINDOW, C_KV_HEADS, C_HEAD_DIM)


def _out_proj_kernel(o_ref, h_ref, w_ref, out_ref):
    out_ref[...] = h_ref[...] + _dot(o_ref[...], w_ref[...])


def _out_proj(o, h, w):
    n, d = h.shape
    tm = min(ROW_TILE, n)
    row = lambda wd: pl.BlockSpec((tm, wd), lambda i: (i, 0))
    return pl.pallas_call(
        _out_proj_kernel,
        grid=(n // tm,),
        in_specs=[row(o.shape[1]), row(d), _const_spec(w.shape)],
        out_specs=row(d),
        out_shape=jax.ShapeDtypeStruct((n, d), F32),
        compiler_params=_params(("parallel",)),
        name="out_proj",
    )(o, h, w)


def kernel(x_prompt, x_sample, cache_a_k, cache_a_v, cache_a_idx, cache_b_ckv, cache_b_krope, state_c_k, state_c_v, page_table, p_prompt, p_sample, rel_table, w_in_even, w_out_even, g_bq, w_buq, g_bkv, w_buk, w_buv, w_in_odd, w_out_odd, c_sinks, g_mix, g_ffn, w_ffn_gate, w_ffn_up, w_ffn_down, g_ple, w_ple_gate, w_ple_proj, g_final):
    b, s, d = x_prompt.shape
    bd, t, _ = x_sample.shape
    depth = g_mix.shape[0]
    assert depth == 2 and w_in_even.shape[0] == 1 and w_in_odd.shape[0] == 1
    assert s % 256 == 0 and rel_table.shape == (REL_BUCKETS, A_HEADS)
    assert t <= QP and bd % GB == 0
    past = page_table.shape[1] * PAGE

    we = _prep_even_weights(w_in_even[0], w_out_even[0], g_bq[0], w_buq[0], g_bkv[0], w_buk[0], w_buv[0])
    w_in_o, w_out_o = _prep_odd_weights(w_in_odd[0], w_out_odd[0])
    lws = [{"g_ffn": g_ffn[i].reshape(1, d), "g_ple": g_ple[i].reshape(1, d),
            "wg": w_ffn_gate[i].astype(BF16), "wu": w_ffn_up[i].astype(BF16), "wd": w_ffn_down[i].astype(BF16),
            "wpg": w_ple_gate[i].astype(BF16), "wpp": w_ple_proj[i].astype(BF16)} for i in range(depth)]
    gfin = g_final.reshape(1, d)
    bias_raw = _bias_window(rel_table)
    bias_rel = bias_raw - rel_table[REL_BUCKETS - 1].astype(F32)[:, None, None]
    cos_p, sin_p = _rope_tables(jnp.arange(s, dtype=jnp.int32), B_HEADS)
    cos_s, sin_s = _rope_tables(jnp.tile(past + jnp.arange(t, dtype=jnp.int32), bd), B_HEADS)

    hp = x_prompt.reshape(b * s, d)
    hs = x_sample.reshape(bd * t, d)
    g0 = g_mix[0].reshape(1, d)
    g1 = g_mix[1].reshape(1, d)

    pe = _even_proj(hp, g0, we, cos_p, sin_p)
    r3 = lambda a: a.reshape(b, s, a.shape[-1])
    oa = _attn_a_prompt(r3(pe["qi"]), r3(pe["wi"]), r3(pe["qa"]), r3(pe["ki2"]), r3(pe["ka_bf"]), r3(pe["va_bf"]),
                        bias_rel)
    obl = _mla_prompt(r3(pe["q_lat"]), r3(pe["q_rope"]), r3(pe["ckv_bf"]), r3(pe["kr_rep"]))
    hp = _even_merge(oa.reshape(b * s, A_WIDTH), obl.reshape(b * s, -1), hp, we)
    hp = _tail(hp, p_prompt[0].reshape(b * s, -1), lws[0], gfin, False)

    se = _even_proj(hs, g0, we, cos_s, sin_s)
    oa_s, obl_s = _sample_even_attention(se, bd, t, cache_a_k, cache_a_v, cache_a_idx, cache_b_ckv, cache_b_krope,
                                         page_table, bias_rel)
    hs = _even_merge(oa_s, obl_s, hs, we)
    hs = _tail(hs, p_sample[0].reshape(bd * t, -1), lws[0], gfin, False)

    q, k, kb, v, vb = _odd_proj(hp, g1, w_in_o)
    hp = _swa_prompt(q.reshape(b, s, -1), kb.reshape(b, s, -1), vb.reshape(b, s, -1), bias_raw, c_sinks[0],
                     hp.reshape(b, s, d), w_out_o).reshape(b * s, d)
    y_prompt = _tail(hp, p_prompt[1].reshape(b * s, -1), lws[1], gfin, True)
    wp = min(WINDOW, s)
    pc_k = k.reshape(b, s, C_KV_HEADS, C_HEAD_DIM)[:, s - wp:]
    pc_v = v.reshape(b, s, C_KV_HEADS, C_HEAD_DIM)[:, s - wp:]

    qs, ks, ksb, vs, vsb = _odd_proj(hs, g1, w_in_o)
    os_, sc_k, sc_v = _sample_odd_attention(qs, ks, ksb, vs, vsb, state_c_k[0], state_c_v[0], c_sinks[0], bias_raw,
                                            bd, t)
    hs = _out_proj(os_, hs, w_out_o)
    y_sample = _tail(hs, p_sample[1].reshape(bd * t, -1), lws[1], gfin, True)

    hd = (A_HEADS, A_HEAD_DIM)
    return (y_prompt.reshape(b, s, d), y_sample.reshape(bd, t, d),
            pe["ka"].reshape(1, b, s, *hd), pe["va"].reshape(1, b, s, *hd), pe["ki"].reshape(1, b, s, IDX_DIM),
            pe["ckv"].reshape(1, b, s, B_KV_LORA), pe["kr"].reshape(1, b, s, B_ROPE),
            pc_k[None], pc_v[None],
            se["ka"].reshape(1, bd, t, *hd), se["va"].reshape(1, bd, t, *hd), se["ki"].reshape(1, bd, t, IDX_DIM),
            se["ckv"].reshape(1, bd, t, B_KV_LORA), se["kr"].reshape(1, bd, t, B_ROPE),
            sc_k[None], sc_v[None])
```

```python
import functools
import math

import numpy as np
import jax
import jax.numpy as jnp
from jax import lax
from jax.experimental import pallas as pl
from jax.experimental.pallas import tpu as pltpu

F32 = jnp.float32
BF16 = jnp.bfloat16

RMS_EPS = 1e-6
A_HEADS, A_HEAD_DIM = 16, 32
A_WIDTH = A_HEADS * A_HEAD_DIM
IDX_HEADS, IDX_DIM = 8, 64
IDX_WIDTH = IDX_HEADS * IDX_DIM
IDX_SCALE = IDX_WIDTH ** -0.5
TOPK_MAX = 256
B_HEADS, B_NOPE, B_ROPE, B_V = 8, 64, 32, 64
B_Q_LORA, B_KV_LORA = 256, 256
B_WIDTH = B_HEADS * B_V
MLA_SCALE = (B_NOPE + B_ROPE) ** -0.5
ROPE_THETA = 10000.0
C_HEADS, C_KV_HEADS, C_HEAD_DIM = 16, 4, 64
C_GROUP = C_HEADS // C_KV_HEADS
C_WIDTH = C_HEADS * C_HEAD_DIM
C_KV_WIDTH = C_KV_HEADS * C_HEAD_DIM
WINDOW = 128
C_SCALE = C_HEAD_DIM ** -0.5
A_SCALE = A_HEAD_DIM ** -0.5
REL_BUCKETS, REL_MAX_DIST = 32, 128
PAGE = 128
QB = 128
LANE = 128
MLA_KB = 512
ROW_TILE = 512
VMEM_LIMIT = 56 * 1024 * 1024

INT_MIN = np.int32(-2 ** 31)
KEY_NEG = np.int32(-2139095041)
NEG_INF = float("-inf")

_NT = (((1,), (1,)), ((), ()))


def _dot(a, b):
    return jnp.dot(a, b, preferred_element_type=F32)


def _dot_nt(a, b):
    return lax.dot_general(a, b, _NT, preferred_element_type=F32)


def _rms(x, g):
    ms = jnp.mean(x * x, axis=-1, keepdims=True)
    return x * lax.rsqrt(ms + RMS_EPS) * g


def _const_spec(shape):
    nd = len(shape)
    return pl.BlockSpec(shape, lambda *_: (0,) * nd, pipeline_mode=pl.Buffered(1))


def _params(sem):
    return pltpu.CompilerParams(dimension_semantics=sem, vmem_limit_bytes=VMEM_LIMIT)


def _rope_apply(x, cos, sin_signed):
    n = x.shape[1]
    lane = lax.broadcasted_iota(jnp.int32, x.shape, 1)
    partner = jnp.where((lane % B_ROPE) < (B_ROPE // 2),
                        pltpu.roll(x, n - B_ROPE // 2, 1), pltpu.roll(x, B_ROPE // 2, 1))
    return x * cos + partner * sin_signed


def _even_proj_kernel(x_ref, g_ref, w_ref, gq_ref, wn_ref, wr_ref, wk_ref, gkv_ref, cos_ref, sin_ref,
                      qa_o, ka_o, kab_o, va_o, vab_o, qi_o, ki_o, ki2_o, wi_o, ql_o, qr_o,
                      ckv_o, ckvb_o, kr_o, krr_o):
    hn = _rms(x_ref[...], g_ref[...]).astype(BF16)
    aw = A_WIDTH
    qa_o[...] = _dot(hn, w_ref[:, 0:aw]).astype(BF16)
    ka = _dot(hn, w_ref[:, aw:2 * aw])
    ka_o[...] = ka
    kab_o[...] = ka.astype(BF16)
    va = _dot(hn, w_ref[:, 2 * aw:3 * aw])
    va_o[...] = va
    vab_o[...] = va.astype(BF16)
    qi_o[...] = _dot(hn, w_ref[:, 3 * aw:4 * aw]).astype(BF16)
    c0 = 4 * aw
    cq = _dot(hn, w_ref[:, c0:c0 + B_Q_LORA])
    ckv = _dot(hn, w_ref[:, c0 + B_Q_LORA:c0 + B_Q_LORA + B_KV_LORA])
    c1 = c0 + B_Q_LORA + B_KV_LORA
    ki2 = _dot(hn, w_ref[:, c1:c1 + 2 * IDX_DIM])
    krr = _dot(hn, w_ref[:, c1 + LANE:c1 + LANE + 256])
    wi = _dot(hn, w_ref[:, c1 + LANE + 256:c1 + 2 * LANE + 256])
    ki_o[...] = ki2[:, 0:IDX_DIM]
    ki2_o[...] = ki2.astype(BF16)
    wi_o[...] = wi[:, 0:IDX_HEADS] * IDX_SCALE
    cos = cos_ref[...]
    sin = sin_ref[...]
    krr = _rope_apply(krr, cos, sin)
    kr_o[...] = krr[:, 0:B_ROPE]
    krr_o[...] = krr.astype(BF16)
    ckvn = _rms(ckv, gkv_ref[...])
    ckv_o[...] = ckvn
    ckvb_o[...] = ckvn.astype(BF16)
    cqn = _rms(cq, gq_ref[...]).astype(BF16)
    qn = _dot(cqn, wn_ref[...]).astype(BF16)
    qr = _dot(cqn, wr_ref[...])
    qr_o[...] = _rope_apply(qr, cos, sin).astype(BF16)
    for p in range(B_HEADS // 2):
        ql_o[:, p * 512:(p + 1) * 512] = _dot(qn[:, p * LANE:(p + 1) * LANE], wk_ref[p]).astype(BF16)


def _even_proj(x, g, wts, cos, sin):
    n, d = x.shape
    tm = min(ROW_TILE, n)
    nblk = n // tm
    tblk = cos.shape[0] // tm
    row = lambda w: pl.BlockSpec((tm, w), lambda i: (i, 0))
    tab = pl.BlockSpec((tm, 256), lambda i: (i % tblk, 0))
    outs = [(A_WIDTH, BF16), (A_WIDTH, F32), (A_WIDTH, BF16), (A_WIDTH, F32), (A_WIDTH, BF16),
            (IDX_WIDTH, BF16), (IDX_DIM, F32), (2 * IDX_DIM, BF16), (IDX_HEADS, F32),
            (B_HEADS * B_KV_LORA, BF16), (B_HEADS * B_ROPE, BF16),
            (B_KV_LORA, F32), (B_KV_LORA, BF16), (B_ROPE, F32), (B_HEADS * B_ROPE, BF16)]
    res = pl.pallas_call(
        _even_proj_kernel,
        grid=(nblk,),
        in_specs=[row(d), _const_spec((1, d)), _const_spec(wts["w_in"].shape),
                  _const_spec((1, B_Q_LORA)), _const_spec(wts["w_qn"].shape), _const_spec(wts["w_qr"].shape),
                  _const_spec(wts["w_bukbd"].shape), _const_spec((1, B_KV_LORA)), tab, tab],
        out_specs=[row(w) for w, _ in outs],
        out_shape=[jax.ShapeDtypeStruct((n, w), dt) for w, dt in outs],
        compiler_params=_params(("parallel",)),
        name="even_proj",
    )(x, g, wts["w_in"], wts["g_bq"], wts["w_qn"], wts["w_qr"], wts["w_bukbd"], wts["g_bkv"], cos, sin)
    names = ["qa", "ka", "ka_bf", "va", "va_bf", "qi", "ki", "ki2", "wi", "q_lat", "q_rope",
             "ckv", "ckv_bf", "kr", "kr_rep"]
    return dict(zip(names, res))


def _sortable_key(x):
    x = jnp.where(x == 0.0, 0.0, x)
    bits = pltpu.bitcast(x, jnp.int32)
    return jnp.where(bits < 0, bits ^ jnp.int32(0x7FFFFFFF), bits)


def _topk_mask(keybuf, maskb, cbuf, npairs, n_sel, idx_bits):
    rows = keybuf.shape[1]
    lane1 = lax.broadcasted_iota(jnp.int32, (rows, LANE), 1)

    def count(pred):
        def body(c, cnt):
            a = jnp.where(pred(keybuf[2 * c], 2 * c), 1.0, 0.0)
            b = jnp.where(pred(keybuf[2 * c + 1], 2 * c + 1), 1.0, 0.0)
            return cnt + (a + b)
        cnt = lax.fori_loop(0, npairs, body, jnp.zeros((rows, LANE), F32))
        return jnp.sum(cnt, axis=1, keepdims=True)

    def bit_step(b, t_u):
        cand_u = t_u | (jnp.int32(1) << (31 - b))
        cbuf[...] = jnp.broadcast_to(cand_u ^ INT_MIN, (rows, LANE))
        tot = count(lambda kk, c: kk >= cbuf[...])
        return jnp.where(tot >= n_sel, cand_u, t_u)

    t_u = lax.fori_loop(0, 32, bit_step, jnp.zeros((rows, 1), jnp.int32))
    t_s = t_u ^ INT_MIN
    tb = jnp.broadcast_to(t_s, (rows, LANE))
    cbuf[...] = tb
    cge = count(lambda kk, c: kk >= cbuf[...])
    tie_rows = jnp.logical_and(t_s > KEY_NEG, cge > n_sel)
    any_tie = jnp.max(jnp.where(tie_rows, 1.0, 0.0)) > 0.0

    @pl.when(jnp.logical_not(any_tie))
    def _():
        def body(c, carry):
            kk = keybuf[c]
            maskb[c] = jnp.where(jnp.logical_and(kk >= cbuf[...], kk > KEY_NEG), 0.0, NEG_INF)
            return carry
        lax.fori_loop(0, 2 * npairs, body, 0)

    @pl.when(any_tie)
    def _():
        need = n_sel - count(lambda kk, c: kk > tb)

        def jstep(b, j_u):
            cand = j_u | (jnp.int32(1) << (idx_bits - 1 - b))
            cb = jnp.broadcast_to(cand, (rows, LANE))
            f = count(lambda kk, c: jnp.logical_and(kk == tb, (c * LANE + lane1) < cb))
            return jnp.where(f < need, cand, j_u)

        j_u = lax.fori_loop(0, idx_bits, jstep, jnp.zeros((rows, 1), jnp.int32))
        jb = jnp.broadcast_to(j_u, (rows, LANE))

        def body(c, carry):
            kk = keybuf[c]
            keep = jnp.logical_or(kk > tb, jnp.logical_and(kk == tb, (c * LANE + lane1) <= jb))
            maskb[c] = jnp.where(jnp.logical_and(keep, kk > KEY_NEG), 0.0, NEG_INF)
            return carry
        lax.fori_loop(0, 2 * npairs, body, 0)


def _attn_a_prompt_kernel(qi_ref, wi_ref, qa_ref, ki2_ref, ka_ref, va_ref, bias_ref, o_ref,
                          qst, wb, keybuf, maskb, cbuf, qstk, mrun, ssum, acc, pbuf, *, n_sel, idx_bits):
    i = pl.program_id(1)
    nk2 = (i + 2) // 2
    lane1 = lax.broadcasted_iota(jnp.int32, (QB, LANE), 1)
    for h in range(IDX_HEADS):
        blk = qi_ref[0, :, (h // 2) * LANE:(h // 2 + 1) * LANE]
        qst[h * QB:(h + 1) * QB, :] = jnp.where((lane1 // IDX_DIM) == (h % 2), blk, jnp.zeros_like(blk))
        wb[h] = jnp.broadcast_to(wi_ref[0, :, h:h + 1], (QB, 2 * LANE))
    row2 = lax.broadcasted_iota(jnp.int32, (QB, 2 * LANE), 0)
    col2 = lax.broadcasted_iota(jnp.int32, (QB, 2 * LANE), 1)

    def score_chunk(c, carry):
        kc = ki2_ref[0, pl.ds(pl.multiple_of(c * 256, 256), 256), :]
        d = _dot_nt(qst[...], kc)
        acc_s = wb[0] * jnp.maximum(d[0:QB, :], 0.0)
        for h in range(1, IDX_HEADS):
            acc_s = acc_s + wb[h] * jnp.maximum(d[h * QB:(h + 1) * QB, :], 0.0)
        key = _sortable_key(acc_s)
        valid = (c * 256 + col2) <= (i * QB + row2)
        key = jnp.where(valid, key, KEY_NEG)
        keybuf[2 * c] = key[:, 0:LANE]
        keybuf[2 * c + 1] = key[:, LANE:2 * LANE]
        return carry

    lax.fori_loop(0, nk2, score_chunk, 0)
    _topk_mask(keybuf, maskb, cbuf, nk2, n_sel, idx_bits)

    lane2 = col2 // A_HEAD_DIM
    nfar = jnp.maximum(nk2 - 2, 0)
    ngrp = A_HEADS // 8
    for g in range(ngrp):
        qg = qa_ref[0, :, g * 256:(g + 1) * 256]
        for hh in range(8):
            qstk[g, hh * QB:(hh + 1) * QB, :] = jnp.where(lane2 == hh, qg, jnp.zeros_like(qg))
    mrun[...] = jnp.full(mrun.shape, NEG_INF, F32)
    ssum[...] = jnp.zeros(ssum.shape, F32)
    acc[...] = jnp.zeros(acc.shape, F32)

    def logits(c, g, hh, s_all, near):
        s = s_all[hh * QB:(hh + 1) * QB, :] * A_SCALE
        halves = []
        for hf in range(2):
            l = s[:, hf * LANE:(hf + 1) * LANE] + maskb[2 * c + hf]
            if near:
                blk = 2 * c + hf
                bh = bias_ref[g * 8 + hh]
                l = l + jnp.where(blk == i, bh[:, LANE:2 * LANE],
                                  jnp.where(blk == i - 1, bh[:, 0:LANE], 0.0))
            halves.append(l)
        return halves

    def pass1(near):
        def body(c, carry):
            rows = pl.ds(pl.multiple_of(c * 256, 256), 256)
            for g in range(ngrp):
                s_all = _dot_nt(qstk[g], ka_ref[0, rows, g * 256:(g + 1) * 256])
                for hh in range(8):
                    l0, l1 = logits(c, g, hh, s_all, near)
                    sl = slice(hh * QB, (hh + 1) * QB)
                    mrun[g, sl, :] = jnp.maximum(mrun[g, sl, :], jnp.maximum(l0, l1))
            return carry
        return body

    lax.fori_loop(0, nfar, pass1(False), 0)
    lax.fori_loop(nfar, nk2, pass1(True), 0)
    for g in range(ngrp):
        mrun[g] = jnp.broadcast_to(jnp.max(mrun[g], axis=1, keepdims=True), mrun.shape[1:])

    def pass2(near):
        def body(c, carry):
            rows = pl.ds(pl.multiple_of(c * 256, 256), 256)
            for g in range(ngrp):
                s_all = _dot_nt(qstk[g], ka_ref[0, rows, g * 256:(g + 1) * 256])
                for hh in range(8):
                    l0, l1 = logits(c, g, hh, s_all, near)
                    sl = slice(hh * QB, (hh + 1) * QB)
                    m = mrun[g, sl, :]
                    p0 = jnp.exp(l0 - m)
                    p1 = jnp.exp(l1 - m)
                    ssum[g, sl, :] = ssum[g, sl, :] + (p0 + p1)
                    pbuf[g, sl, 0:LANE] = p0.astype(BF16)
                    pbuf[g, sl, LANE:2 * LANE] = p1.astype(BF16)
                acc[g] = acc[g] + _dot(pbuf[g], va_ref[0, rows, g * 256:(g + 1) * 256])
            return carry
        return body

    lax.fori_loop(0, nfar, pass2(False), 0)
    lax.fori_loop(nfar, nk2, pass2(True), 0)
    for g in range(ngrp):
        out_g = jnp.zeros((QB, 256), F32)
        for hh in range(8):
            sl = slice(hh * QB, (hh + 1) * QB)
            den = jnp.sum(ssum[g, sl, :], axis=1, keepdims=True)
            out_g = jnp.where(lane2 == hh, acc[g, sl, :] / den, out_g)
        o_ref[0, :, g * 256:(g + 1) * 256] = out_g.astype(BF16)


def _attn_a_prompt(qi, wi, qa, ki2, ka, va, bias_win):
    b, s, _ = qa.shape
    nq = s // QB
    n_sel = min(TOPK_MAX, s // 4)
    idx_bits = int(math.log2(s)) + 1
    kern = functools.partial(_attn_a_prompt_kernel, n_sel=n_sel, idx_bits=idx_bits)
    qblk = lambda w: pl.BlockSpec((1, QB, w), lambda bi, i: (bi, i, 0))
    full = lambda w: pl.BlockSpec((1, s, w), lambda bi, i: (bi, 0, 0))
    return pl.pallas_call(
        kern,
        grid=(b, nq),
        in_specs=[qblk(IDX_WIDTH), qblk(IDX_HEADS), qblk(A_WIDTH), full(2 * IDX_DIM), full(A_WIDTH), full(A_WIDTH),
                  _const_spec(bias_win.shape)],
        out_specs=qblk(A_WIDTH),
        out_shape=jax.ShapeDtypeStruct((b, s, A_WIDTH), BF16),
        scratch_shapes=[pltpu.VMEM((IDX_HEADS * QB, LANE), BF16),
                        pltpu.VMEM((IDX_HEADS, QB, 2 * LANE), F32),
                        pltpu.VMEM((nq, QB, LANE), jnp.int32),
                        pltpu.VMEM((nq, QB, LANE), F32),
                        pltpu.VMEM((QB, LANE), jnp.int32),
                        pltpu.VMEM((A_HEADS // 8, 8 * QB, 256), BF16),
                        pltpu.VMEM((A_HEADS // 8, 8 * QB, LANE), F32),
                        pltpu.VMEM((A_HEADS // 8, 8 * QB, LANE), F32),
                        pltpu.VMEM((A_HEADS // 8, 8 * QB, 256), F32),
                        pltpu.VMEM((A_HEADS // 8, 8 * QB, 256), BF16)],
        compiler_params=_params(("parallel", "arbitrary")),
        name="attn_a_prompt",
    )(qi, wi, qa, ki2, ka, va, bias_win)


def _mla_prompt_kernel(ql_ref, qr_ref, ckv_ref, kr_ref, o_ref, qst, m_scr, l_scr, acc, pbuf, s0, s1, *, kb, nsub):
    i = pl.program_id(1)
    qbm = nsub * QB
    nkb = (i * qbm + qbm - 1) // kb + 1
    lane = lax.broadcasted_iota(jnp.int32, (QB, 256), 1) // B_ROPE
    for sub in range(nsub):
        qr = qr_ref[0, sub * QB:(sub + 1) * QB, :]
        for h in range(B_HEADS):
            qst[sub, h * QB:(h + 1) * QB, 0:256] = ql_ref[0, sub * QB:(sub + 1) * QB, h * 256:(h + 1) * 256]
            qst[sub, h * QB:(h + 1) * QB, 256:512] = jnp.where(lane == h, qr, jnp.zeros_like(qr))
    m_scr[...] = jnp.full(m_scr.shape, NEG_INF, F32)
    l_scr[...] = jnp.zeros(l_scr.shape, F32)
    acc[...] = jnp.zeros(acc.shape, F32)

    def keys(c):
        return pl.ds(pl.multiple_of(c * kb, kb), kb)

    def qk(c, sbuf):
        ckv = ckv_ref[0, keys(c), :]
        kr = kr_ref[0, keys(c), :]
        for sub in range(nsub):
            sbuf[sub] = _dot_nt(qst[sub, :, 0:256], ckv) + _dot_nt(qst[sub, :, 256:512], kr)

    def soft_pv(c, sbuf, diag):
        ckv = ckv_ref[0, keys(c), :]
        for sub in range(nsub):
            if diag:
                qpos = i * qbm + sub * QB + lax.broadcasted_iota(jnp.int32, (QB, kb), 0)
                kpos = c * kb + lax.broadcasted_iota(jnp.int32, (QB, kb), 1)
                ok = kpos <= qpos
            for h in range(B_HEADS):
                sl = slice(h * QB, (h + 1) * QB)
                s = sbuf[sub, sl, :] * MLA_SCALE
                if diag:
                    s = jnp.where(ok, s, NEG_INF)
                m_old = m_scr[sub, sl, :]
                m_new = jnp.maximum(m_old, jnp.max(s, axis=1, keepdims=True))
                alpha = jnp.exp(m_old - m_new)
                p = jnp.exp(s - m_new)
                l_scr[sub, sl, :] = alpha * l_scr[sub, sl, :] + jnp.sum(p, axis=1, keepdims=True)
                acc[sub, sl, :] = alpha * acc[sub, sl, :]
                m_scr[sub, sl, :] = m_new
                pbuf[sub, sl, :] = p.astype(BF16)
            acc[sub] = acc[sub] + _dot(pbuf[sub], ckv)

    qk(0, s0)

    def pair(t, carry):
        c = 2 * t
        qk(c + 1, s1)
        soft_pv(c, s0, False)
        qk(c + 2, s0)
        soft_pv(c + 1, s1, False)
        return carry

    lax.fori_loop(0, (nkb - 1) // 2, pair, 0)
    even = (nkb - 1) % 2 == 1

    @pl.when(even)
    def _():
        qk(nkb - 1, s1)
        soft_pv(nkb - 2, s0, False)
        soft_pv(nkb - 1, s1, True)

    @pl.when(jnp.logical_not(even))
    def _():
        soft_pv(nkb - 1, s0, True)

    for sub in range(nsub):
        for h in range(B_HEADS):
            sl = slice(h * QB, (h + 1) * QB)
            o_ref[0, sub * QB:(sub + 1) * QB, h * 256:(h + 1) * 256] = (
                acc[sub, sl, :] / l_scr[sub, sl, :]).astype(BF16)


def _mla_prompt(q_lat, q_rope, ckv, kr_rep):
    b, s, _ = q_lat.shape
    kb = min(MLA_KB, s)
    nsub = 2
    qbm = nsub * QB
    kern = functools.partial(_mla_prompt_kernel, kb=kb, nsub=nsub)
    qblk = lambda w: pl.BlockSpec((1, qbm, w), lambda bi, i: (bi, i, 0))
    full = lambda w: pl.BlockSpec((1, s, w), lambda bi, i: (bi, 0, 0))
    rows = B_HEADS * QB
    return pl.pallas_call(
        kern,
        grid=(b, s // qbm),
        in_specs=[qblk(B_HEADS * B_KV_LORA), qblk(B_HEADS * B_ROPE), full(B_KV_LORA), full(B_HEADS * B_ROPE)],
        out_specs=qblk(B_HEADS * B_KV_LORA),
        out_shape=jax.ShapeDtypeStruct((b, s, B_HEADS * B_KV_LORA), BF16),
        scratch_shapes=[pltpu.VMEM((nsub, rows, 512), BF16),
                        pltpu.VMEM((nsub, rows, 1), F32),
                        pltpu.VMEM((nsub, rows, 1), F32),
                        pltpu.VMEM((nsub, rows, B_KV_LORA), F32),
                        pltpu.VMEM((nsub, rows, kb), BF16),
                        pltpu.VMEM((nsub, rows, kb), F32),
                        pltpu.VMEM((nsub, rows, kb), F32)],
        compiler_params=_params(("parallel", "arbitrary")),
        name="mla_prompt",
    )(q_lat, q_rope, ckv, kr_rep)


def _even_merge_kernel(oa_ref, obl_ref, h_ref, wbuv_ref, wout_ref, o_ref):
    obs = [_dot(obl_ref[:, p * 512:(p + 1) * 512], wbuv_ref[p]).astype(BF16) for p in range(B_HEADS // 2)]
    ob = jnp.concatenate(obs, axis=1)
    o_ref[...] = (h_ref[...] + _dot(oa_ref[...], wout_ref[0:A_WIDTH, :])
                  + _dot(ob, wout_ref[A_WIDTH:A_WIDTH + B_WIDTH, :]))


def _even_merge(oa, obl, h, wts):
    n, d = h.shape
    tm = min(ROW_TILE, n)
    row = lambda w: pl.BlockSpec((tm, w), lambda i: (i, 0))
    return pl.pallas_call(
        _even_merge_kernel,
        grid=(n // tm,),
        in_specs=[row(A_WIDTH), row(B_HEADS * B_KV_LORA), row(d),
                  _const_spec(wts["w_buvbd"].shape), _const_spec(wts["w_out_even"].shape)],
        out_specs=row(d),
        out_shape=jax.ShapeDtypeStruct((n, d), F32),
        compiler_params=_params(("parallel",)),
        name="even_merge",
    )(oa, obl, h, wts["w_buvbd"], wts["w_out_even"])


def _tail_kernel(h_ref, p_ref, gffn_ref, gple_ref, wg_ref, wu_ref, wd_ref, wpg_ref, wpp_ref, gfin_ref, o_ref,
                 *, final, fc):
    h = h_ref[...]
    hn = _rms(h, gffn_ref[...]).astype(BF16)
    dff = wg_ref.shape[1]
    acc = jnp.zeros(h.shape, F32)
    for c in range(dff // fc):
        g = _dot(hn, wg_ref[:, c * fc:(c + 1) * fc])
        u = _dot(hn, wu_ref[:, c * fc:(c + 1) * fc])
        a = (g * jax.nn.sigmoid(g) * u).astype(BF16)
        acc = acc + _dot(a, wd_ref[c * fc:(c + 1) * fc, :])
    h2 = h + acc
    gate = jax.nn.sigmoid(_dot(_rms(h2, gple_ref[...]).astype(BF16), wpg_ref[...]))
    h3 = h2 + gate * _dot(p_ref[...].astype(BF16), wpp_ref[...])
    if final:
        h3 = _rms(h3, gfin_ref[...])
    o_ref[...] = h3


def _tail(h, p, lw, g_final, final):
    n, d = h.shape
    tm = min(ROW_TILE, n)
    row = lambda w: pl.BlockSpec((tm, w), lambda i: (i, 0))
    kern = functools.partial(_tail_kernel, final=final, fc=256)
    return pl.pallas_call(
        kern,
        grid=(n // tm,),
        in_specs=[row(d), row(p.shape[1]), _const_spec((1, d)), _const_spec((1, d)),
                  _const_spec(lw["wg"].shape), _const_spec(lw["wu"].shape), _const_spec(lw["wd"].shape),
                  _const_spec(lw["wpg"].shape), _const_spec(lw["wpp"].shape), _const_spec((1, d))],
        out_specs=row(d),
        out_shape=jax.ShapeDtypeStruct((n, d), F32),
        compiler_params=_params(("parallel",)),
        name="layer_tail",
    )(h, p, lw["g_ffn"], lw["g_ple"], lw["wg"], lw["wu"], lw["wd"], lw["wpg"], lw["wpp"], g_final)


def _odd_proj_kernel(x_ref, g_ref, w_ref, q_o, k_o, kb_o, v_o, vb_o):
    hn = _rms(x_ref[...], g_ref[...]).astype(BF16)
    q_o[...] = _dot(hn, w_ref[:, 0:C_WIDTH]).astype(BF16)
    k = _dot(hn, w_ref[:, C_WIDTH:C_WIDTH + C_KV_WIDTH])
    v = _dot(hn, w_ref[:, C_WIDTH + C_KV_WIDTH:C_WIDTH + 2 * C_KV_WIDTH])
    k_o[...] = k
    kb_o[...] = k.astype(BF16)
    v_o[...] = v
    vb_o[...] = v.astype(BF16)


def _odd_proj(x, g, w):
    n, d = x.shape
    tm = min(ROW_TILE, n)
    row = lambda wd: pl.BlockSpec((tm, wd), lambda i: (i, 0))
    outs = [(C_WIDTH, BF16), (C_KV_WIDTH, F32), (C_KV_WIDTH, BF16), (C_KV_WIDTH, F32), (C_KV_WIDTH, BF16)]
    return pl.pallas_call(
        _odd_proj_kernel,
        grid=(n // tm,),
        in_specs=[row(d), _const_spec((1, d)), _const_spec(w.shape)],
        out_specs=[row(wd) for wd, _ in outs],
        out_shape=[jax.ShapeDtypeStruct((n, wd), dt) for wd, dt in outs],
        compiler_params=_params(("parallel",)),
        name="odd_proj",
    )(x, g, w)


def _swa_prompt_kernel(sink_ref, q_ref, kp_ref, kc_ref, vp_ref, vc_ref, bias_ref, h_ref, wout_ref, o_ref, *, nsub):
    i = pl.program_id(1)
    kall = jnp.concatenate([kp_ref[0], kc_ref[0]], axis=0)
    vall = jnp.concatenate([vp_ref[0], vc_ref[0]], axis=0)
    r = lax.broadcasted_iota(jnp.int32, (QB, 2 * WINDOW), 0)
    jj = lax.broadcasted_iota(jnp.int32, (QB, 2 * WINDOW), 1)
    dist = WINDOW + r - jj
    band = (dist >= 0) & (dist <= WINDOW)
    lane = jj // C_HEAD_DIM
    rows_out = []
    for sub in range(nsub):
        kk = kall[sub * QB:sub * QB + 2 * WINDOW, :]
        vv = vall[sub * QB:sub * QB + 2 * WINDOW, :]
        mask = band & ((jj >= WINDOW) | (i > 0)) if sub == 0 else band
        outs = []
        for g in range(C_GROUP):
            qg = q_ref[0, sub * QB:(sub + 1) * QB, g * 256:(g + 1) * 256]
            og = jnp.zeros((QB, 256), F32)
            for k in range(C_KV_HEADS):
                hidx = k * C_GROUP + g
                qm = jnp.where(lane == k, qg, jnp.zeros_like(qg))
                l = _dot_nt(qm, kk) * C_SCALE + bias_ref[hidx]
                l = jnp.where(mask, l, NEG_INF)
                sk = sink_ref[hidx]
                m = jnp.maximum(jnp.max(l, axis=1, keepdims=True), sk)
                e = jnp.exp(l - m)
                den = jnp.sum(e, axis=1, keepdims=True) + jnp.exp(sk - m)
                og = jnp.where(lane == k, _dot((e / den).astype(BF16), vv), og)
            outs.append(og.astype(BF16))
        rows_out.append(jnp.concatenate(outs, axis=1))
    o_all = jnp.concatenate(rows_out, axis=0)
    o_ref[0] = h_ref[0] + _dot(o_all, wout_ref[...])


def _swa_prompt(q, k, v, bias_raw, sinks, h, w_out):
    b, s, d = h.shape
    nsub = 2
    qbs = nsub * QB
    nq = s // qbs
    cur = lambda w: pl.BlockSpec((1, qbs, w), lambda bi, i: (bi, i, 0))
    prev = lambda w: pl.BlockSpec((1, QB, w), lambda bi, i: (bi, jnp.maximum(nsub * i - 1, 0), 0))
    return pl.pallas_call(
        functools.partial(_swa_prompt_kernel, nsub=nsub),
        grid=(b, nq),
        in_specs=[pl.BlockSpec(memory_space=pltpu.SMEM),
                  cur(C_WIDTH), prev(C_KV_WIDTH), cur(C_KV_WIDTH), prev(C_KV_WIDTH), cur(C_KV_WIDTH),
                  _const_spec(bias_raw.shape), cur(d), _const_spec(w_out.shape)],
        out_specs=cur(d),
        out_shape=jax.ShapeDtypeStruct((b, s, d), F32),
        compiler_params=_params(("parallel", "parallel")),
        name="swa_prompt",
    )(sinks, q, k, k, v, v, bias_raw, h, w_out)


def _rel_bucket(dist):
    n = jnp.maximum(dist, 0)
    max_exact = REL_BUCKETS // 2
    nf = jnp.maximum(n, 1).astype(F32)
    large = max_exact + (jnp.log(nf / max_exact) / math.log(REL_MAX_DIST / max_exact)
                         * (REL_BUCKETS - max_exact)).astype(jnp.int32)
    large = jnp.minimum(large, REL_BUCKETS - 1)
    return jnp.where(n < max_exact, n, large)


def _bias_window(rel_table):
    r = jnp.arange(QB)[:, None]
    j = jnp.arange(2 * QB)[None, :]
    bucket = _rel_bucket(WINDOW + r - j)
    onehot = (bucket[None] == jnp.arange(REL_BUCKETS)[:, None, None]).astype(F32)
    return jnp.einsum('bh,brj->hrj', rel_table.astype(F32), onehot, precision=lax.Precision.HIGHEST)


def _rope_tables(pos, reps):
    half = B_ROPE // 2
    inv = ROPE_THETA ** (-jnp.arange(half, dtype=F32) / half)
    ang = pos.astype(F32)[:, None] * inv
    cos, sin = jnp.cos(ang), jnp.sin(ang)
    cos_f = jnp.tile(jnp.concatenate([cos, cos], axis=1), (1, reps))
    sin_f = jnp.tile(jnp.concatenate([-sin, sin], axis=1), (1, reps))
    return cos_f, sin_f


def _block_diag_pairs(m):
    z = jnp.zeros_like(m[0])
    return jnp.stack([jnp.concatenate([jnp.concatenate([m[2 * p], z], axis=1),
                                       jnp.concatenate([z, m[2 * p + 1]], axis=1)], axis=0)
                      for p in range(m.shape[0] // 2)])


def _prep_even_weights(w_in, w_out, g_bq, w_buq, g_bkv, w_buk, w_buv):
    d = w_in.shape[0]
    splits = np.cumsum([A_WIDTH, A_WIDTH, A_WIDTH, IDX_WIDTH, IDX_DIM, IDX_HEADS, B_Q_LORA, B_KV_LORA])
    qa, ka, va, qi, ki, wi, cq, ckv, kr = jnp.split(w_in, splits.tolist(), axis=1)
    wi_pad = jnp.concatenate([wi, jnp.zeros((d, LANE - IDX_HEADS), w_in.dtype)], axis=1)
    w_all = jnp.concatenate([qa, ka, va, qi, cq, ckv, ki, ki, jnp.tile(kr, (1, B_HEADS)), wi_pad], axis=1)
    wq = w_buq.reshape(B_Q_LORA, B_HEADS, B_NOPE + B_ROPE)
    w_qn = wq[:, :, :B_NOPE].reshape(B_Q_LORA, B_HEADS * B_NOPE)
    w_qr = wq[:, :, B_NOPE:].reshape(B_Q_LORA, B_HEADS * B_ROPE)
    w_bukt = jnp.transpose(w_buk, (1, 2, 0))
    w_buvh = jnp.transpose(w_buv, (1, 0, 2))
    return {
        "w_in": w_all.astype(BF16),
        "g_bq": g_bq.reshape(1, -1), "g_bkv": g_bkv.reshape(1, -1),
        "w_qn": w_qn.astype(BF16), "w_qr": w_qr.astype(BF16),
        "w_bukbd": _block_diag_pairs(w_bukt).astype(BF16),
        "w_buvbd": _block_diag_pairs(w_buvh).astype(BF16),
        "w_out_even": w_out.astype(BF16),
    }


def _prep_odd_weights(w_in, w_out):
    d = w_in.shape[0]
    q = w_in[:, :C_WIDTH].reshape(d, C_KV_HEADS, C_GROUP, C_HEAD_DIM)
    q = jnp.transpose(q, (0, 2, 1, 3)).reshape(d, C_WIDTH)
    w_in_p = jnp.concatenate([q, w_in[:, C_WIDTH:]], axis=1).astype(BF16)
    wo = w_out.reshape(C_KV_HEADS, C_GROUP, C_HEAD_DIM, -1)
    wo = jnp.transpose(wo, (1, 0, 2, 3)).reshape(C_WIDTH, -1).astype(BF16)
    return w_in_p, wo


QP = 8
GB = LANE // QP


def _pages_per_step(n_pages, want):
    return want if n_pages % want == 0 else n_pages


def _page_index(b, j, pt_ref, *, p, pps):
    return (0, pt_ref[b, j * pps + p], 0, 0)


def _page_specs(rows, width, pps):
    return [pl.BlockSpec((None, None, rows, width), functools.partial(_page_index, p=p, pps=pps))
            for p in range(pps)]


def _slot_minor(pool):
    nd = pool.ndim
    t = jnp.transpose(pool, (0, 1) + tuple(range(3, nd)) + (2,))
    return t.reshape(pool.shape[0], pool.shape[1], -1, pool.shape[2])


def _new_key_ok(rows, bloc, t):
    row = lax.broadcasted_iota(jnp.int32, (rows, LANE), 0)
    col = lax.broadcasted_iota(jnp.int32, (rows, LANE), 1)
    return (col // QP == bloc) & (col % QP < t) & (col % QP <= row % QP)


def _online_update(s, pv, m_scr, l_scr, acc):
    m_old = m_scr[...]
    m_new = jnp.maximum(m_old, jnp.max(s, axis=1, keepdims=True))
    m_safe = jnp.where(m_new == NEG_INF, 0.0, m_new)
    alpha = jnp.exp(m_old - m_safe)
    p = jnp.exp(s - m_safe)
    l_scr[...] = alpha * l_scr[...] + jnp.sum(p, axis=1, keepdims=True)
    acc[...] = alpha * acc[...] + pv(p.astype(BF16))
    m_scr[...] = m_new


def _idx_scores_kernel(pt_ref, q_ref, w_ref, *rest, pps):
    pages, o_ref, kcat = rest[:pps], rest[pps], rest[pps + 1]
    for p in range(pps):
        kcat[:, p * PAGE:(p + 1) * PAGE] = pages[p][...].astype(BF16)
    d = _dot(q_ref[0], kcat[...])
    for p in range(pps):
        cols = slice(p * PAGE, (p + 1) * PAGE)
        a = w_ref[0, 0] * jnp.maximum(d[0:QP, cols], 0.0)
        for h in range(1, IDX_HEADS):
            a = a + w_ref[0, h] * jnp.maximum(d[h * QP:(h + 1) * QP, cols], 0.0)
        o_ref[0, :, cols] = a


def _idx_scores_sample(page_table, qi_s, w_s, idx_t):
    bd, n_pages = page_table.shape
    pps = _pages_per_step(n_pages, 32)
    kern = functools.partial(_idx_scores_kernel, pps=pps)
    grid_spec = pltpu.PrefetchScalarGridSpec(
        num_scalar_prefetch=1, grid=(bd, n_pages // pps),
        in_specs=[pl.BlockSpec((1, IDX_HEADS * QP, IDX_DIM), lambda b, j, pt: (b, 0, 0)),
                  pl.BlockSpec((1, IDX_HEADS, QP, LANE), lambda b, j, pt: (b, 0, 0, 0))]
                 + _page_specs(IDX_DIM, PAGE, pps),
        out_specs=pl.BlockSpec((1, QP, pps * PAGE), lambda b, j, pt: (b, 0, j)),
        scratch_shapes=[pltpu.VMEM((IDX_DIM, pps * PAGE), BF16)])
    return pl.pallas_call(
        kern, grid_spec=grid_spec,
        out_shape=jax.ShapeDtypeStruct((bd, QP, n_pages * PAGE), F32),
        compiler_params=_params(("parallel", "arbitrary")),
        name="idx_scores_sample",
    )(page_table, qi_s, w_s, *([idx_t] * pps))


def _select_sample_kernel(sc_ref, qi_ref, wi_ref, kin_ref, mbp_ref, mbn_ref, keybuf, maskb, cbuf,
                          *, n_sel, idx_bits, npast, t):
    rows = sc_ref.shape[0]
    row = lax.broadcasted_iota(jnp.int32, (rows, LANE), 0)
    lane1 = lax.broadcasted_iota(jnp.int32, (rows, LANE), 1)
    real = (row % QP) < t
    for c in range(npast):
        keybuf[c] = jnp.where(real, _sortable_key(sc_ref[:, c * LANE:(c + 1) * LANE]), KEY_NEG)
    a = jnp.zeros((rows, LANE), F32)
    for h in range(IDX_HEADS):
        blk = qi_ref[:, (h // 2) * LANE:(h // 2 + 1) * LANE]
        qm = jnp.where((lane1 // IDX_DIM) == (h % 2), blk, jnp.zeros_like(blk))
        a = a + wi_ref[:, h:h + 1] * jnp.maximum(_dot_nt(qm, kin_ref[...]), 0.0)
    ok = jnp.logical_and(_new_key_ok(rows, row // QP, t), real)
    keybuf[npast] = jnp.where(ok, _sortable_key(a), KEY_NEG)
    keybuf[npast + 1] = jnp.full((rows, LANE), KEY_NEG, jnp.int32)
    _topk_mask(keybuf, maskb, cbuf, (npast + 2) // 2, n_sel, idx_bits)
    for c in range(npast):
        mbp_ref[:, c * LANE:(c + 1) * LANE] = maskb[c]
    mbn_ref[...] = maskb[npast]


def _select_sample(scores, qi8, wi8, kin8, t):
    rows, past = scores.shape
    npast = past // LANE
    assert npast % 2 == 0
    n_sel = min(TOPK_MAX, (past + t) // 4)
    idx_bits = int(math.ceil(math.log2((npast + 2) * LANE))) + 1
    kern = functools.partial(_select_sample_kernel, n_sel=n_sel, idx_bits=idx_bits, npast=npast, t=t)
    rb = lambda w: pl.BlockSpec((LANE, w), lambda i: (i, 0))
    return pl.pallas_call(
        kern, grid=(rows // LANE,),
        in_specs=[rb(past), rb(IDX_WIDTH), rb(IDX_HEADS), rb(2 * IDX_DIM)],
        out_specs=[rb(past), rb(LANE)],
        out_shape=[jax.ShapeDtypeStruct((rows, past), F32), jax.ShapeDtypeStruct((rows, LANE), F32)],
        scratch_shapes=[pltpu.VMEM((npast + 2, LANE, LANE), jnp.int32),
                        pltpu.VMEM((npast + 2, LANE, LANE), F32),
                        pltpu.VMEM((LANE, LANE), jnp.int32)],
        compiler_params=_params(("parallel",)),
        name="select_sample",
    )(scores, qi8, wi8, kin8)


def _attn_a_sample_kernel(pt_ref, q_ref, mb_ref, bias_ref, mbn_ref, biasn_ref, kn_ref, vn_ref, *rest, pps):
    kpages, vpages, o_ref = rest[:pps], rest[pps:2 * pps], rest[2 * pps]
    kcat, vcat, m_scr, l_scr, acc = rest[2 * pps + 1:]
    j = pl.program_id(1)
    rows = A_HEADS * QP

    @pl.when(j == 0)
    def _():
        m_scr[...] = jnp.full(m_scr.shape, NEG_INF, F32)
        l_scr[...] = jnp.zeros(l_scr.shape, F32)
        acc[...] = jnp.zeros(acc.shape, F32)

    for p in range(pps):
        kcat[:, p * PAGE:(p + 1) * PAGE] = kpages[p][...].astype(BF16)
        vcat[:, p * PAGE:(p + 1) * PAGE] = vpages[p][...].astype(BF16)
    n = pps * PAGE
    q = q_ref[0]
    s = _dot(q, kcat[...]) * A_SCALE + bias_ref[0]
    s = (s.reshape(A_HEADS, QP, n) + mb_ref[0][None]).reshape(rows, n)
    _online_update(s, lambda p: _dot_nt(p, vcat[...]), m_scr, l_scr, acc)

    @pl.when(j == pl.num_programs(1) - 1)
    def _():
        sn = _dot_nt(q, kn_ref[...]) * A_SCALE + biasn_ref[...]
        sn = (sn.reshape(A_HEADS, QP, LANE) + mbn_ref[...][None]).reshape(rows, LANE)
        _online_update(sn, lambda p: _dot(p, vn_ref[...]), m_scr, l_scr, acc)
        l = l_scr[...]
        o = acc[...] / jnp.where(l == 0.0, 1.0, l)
        lane = lax.broadcasted_iota(jnp.int32, (QP, A_WIDTH), 1) // A_HEAD_DIM
        out = jnp.zeros((QP, A_WIDTH), F32)
        for h in range(A_HEADS):
            out = jnp.where(lane == h, o[h * QP:(h + 1) * QP, :], out)
        o_ref[0] = out.astype(BF16)


def _attn_a_sample(page_table, qa_s, mb_past, bias_pages, mb_new, bias_new, kn8, vn8, cache_k, cache_v, pps):
    bd, n_pages = page_table.shape
    n = pps * PAGE
    rows = A_HEADS * QP
    kern = functools.partial(_attn_a_sample_kernel, pps=pps)
    grid_spec = pltpu.PrefetchScalarGridSpec(
        num_scalar_prefetch=1, grid=(bd, n_pages // pps),
        in_specs=[pl.BlockSpec((1, rows, A_WIDTH), lambda b, j, pt: (b, 0, 0)),
                  pl.BlockSpec((1, QP, n), lambda b, j, pt: (b, 0, j)),
                  pl.BlockSpec((1, rows, n), lambda b, j, pt: (j, 0, 0)),
                  pl.BlockSpec((QP, LANE), lambda b, j, pt: (b, 0)),
                  pl.BlockSpec((rows, LANE), lambda b, j, pt: (0, 0)),
                  pl.BlockSpec((LANE, A_WIDTH), lambda b, j, pt: (b // GB, 0)),
                  pl.BlockSpec((LANE, A_WIDTH), lambda b, j, pt: (b // GB, 0))]
                 + _page_specs(A_WIDTH, PAGE, pps) + _page_specs(A_WIDTH, PAGE, pps),
        out_specs=pl.BlockSpec((1, QP, A_WIDTH), lambda b, j, pt: (b, 0, 0)),
        scratch_shapes=[pltpu.VMEM((A_WIDTH, n), BF16), pltpu.VMEM((A_WIDTH, n), BF16),
                        pltpu.VMEM((rows, 1), F32), pltpu.VMEM((rows, 1), F32), pltpu.VMEM((rows, A_WIDTH), F32)])
    return pl.pallas_call(
        kern, grid_spec=grid_spec,
        out_shape=jax.ShapeDtypeStruct((bd, QP, A_WIDTH), BF16),
        compiler_params=_params(("parallel", "arbitrary")),
        name="attn_a_sample",
    )(page_table, qa_s, mb_past, bias_pages, mb_new, bias_new, kn8, vn8, *([cache_k] * pps), *([cache_v] * pps))


def _mla_sample_kernel(pt_ref, ql_ref, qr_ref, cn_ref, rn_ref, *rest, pps, t):
    cpages, rpages, o_ref = rest[:pps], rest[pps:2 * pps], rest[2 * pps]
    ccat, rcat, m_scr, l_scr, acc = rest[2 * pps + 1:]
    b = pl.program_id(0)
    j = pl.program_id(1)
    rows = B_HEADS * QP

    @pl.when(j == 0)
    def _():
        m_scr[...] = jnp.full(m_scr.shape, NEG_INF, F32)
        l_scr[...] = jnp.zeros(l_scr.shape, F32)
        acc[...] = jnp.zeros(acc.shape, F32)

    for p in range(pps):
        ccat[p * PAGE:(p + 1) * PAGE, :] = cpages[p][...].astype(BF16)
        rcat[:, p * PAGE:(p + 1) * PAGE] = rpages[p][...].astype(BF16)
    ql, qr = ql_ref[0], qr_ref[0]
    s = (_dot_nt(ql, ccat[...]) + _dot(qr, rcat[...])) * MLA_SCALE
    _online_update(s, lambda p: _dot(p, ccat[...]), m_scr, l_scr, acc)

    @pl.when(j == pl.num_programs(1) - 1)
    def _():
        cn = cn_ref[...]
        sn = (_dot_nt(ql, cn) + _dot_nt(qr, rn_ref[...])) * MLA_SCALE
        sn = jnp.where(_new_key_ok(rows, b % GB, t), sn, NEG_INF)
        _online_update(sn, lambda p: _dot(p, cn), m_scr, l_scr, acc)
        o_ref[0] = (acc[...] / l_scr[...]).astype(BF16)


def _mla_sample(page_table, ql_s, qr_s, cn8, rn8, cache_ckv, kr_t, t):
    bd, n_pages = page_table.shape
    pps = _pages_per_step(n_pages, 16)
    n = pps * PAGE
    rows = B_HEADS * QP
    kern = functools.partial(_mla_sample_kernel, pps=pps, t=t)
    grid_spec = pltpu.PrefetchScalarGridSpec(
        num_scalar_prefetch=1, grid=(bd, n_pages // pps),
        in_specs=[pl.BlockSpec((1, rows, B_KV_LORA), lambda b, j, pt: (b, 0, 0)),
                  pl.BlockSpec((1, rows, B_ROPE), lambda b, j, pt: (b, 0, 0)),
                  pl.BlockSpec((LANE, B_KV_LORA), lambda b, j, pt: (b // GB, 0)),
                  pl.BlockSpec((LANE, B_ROPE), lambda b, j, pt: (b // GB, 0))]
                 + _page_specs(PAGE, B_KV_LORA, pps) + _page_specs(B_ROPE, PAGE, pps),
        out_specs=pl.BlockSpec((1, rows, B_KV_LORA), lambda b, j, pt: (b, 0, 0)),
        scratch_shapes=[pltpu.VMEM((n, B_KV_LORA), BF16), pltpu.VMEM((B_ROPE, n), BF16),
                        pltpu.VMEM((rows, 1), F32), pltpu.VMEM((rows, 1), F32), pltpu.VMEM((rows, B_KV_LORA), F32)])
    return pl.pallas_call(
        kern, grid_spec=grid_spec,
        out_shape=jax.ShapeDtypeStruct((bd, rows, B_KV_LORA), BF16),
        compiler_params=_params(("parallel", "arbitrary")),
        name="mla_sample",
    )(page_table, ql_s, qr_s, cn8, rn8, *([cache_ckv] * pps), *([kr_t] * pps))


def _swa_sample_kernel(q_ref, bk_ref, bv_ref, kn_ref, vn_ref, ks_ref, vs_ref, bb_ref, bn_ref, sk_ref,
                       o_ref, nk_ref, nv_ref, *, t):
    rows = C_HEADS * QP
    row = lax.broadcasted_iota(jnp.int32, (rows, LANE), 0)
    col = lax.broadcasted_iota(jnp.int32, (rows, LANE), 1)
    buf_ok = col >= row % QP
    lane = lax.broadcasted_iota(jnp.int32, (QP, C_KV_WIDTH), 1) // C_HEAD_DIM
    row8 = lax.broadcasted_iota(jnp.int32, (QP, C_KV_WIDTH), 0)
    sk = sk_ref[:, 0:1]
    kn, vn = kn_ref[...], vn_ref[...]
    for bi in range(GB):
        q = q_ref[bi]
        bk, bv = bk_ref[bi], bv_ref[bi]
        lb = jnp.where(buf_ok, _dot_nt(q, bk.astype(BF16)) * C_SCALE + bb_ref[...], NEG_INF)
        ln = jnp.where(_new_key_ok(rows, bi, t), _dot_nt(q, kn) * C_SCALE + bn_ref[...], NEG_INF)
        m = jnp.maximum(jnp.maximum(jnp.max(lb, axis=1, keepdims=True), jnp.max(ln, axis=1, keepdims=True)), sk)
        eb = jnp.exp(lb - m)
        en = jnp.exp(ln - m)
        den = jnp.sum(eb, axis=1, keepdims=True) + jnp.sum(en, axis=1, keepdims=True) + jnp.exp(sk - m)
        o = _dot((eb / den).astype(BF16), bv.astype(BF16)) + _dot((en / den).astype(BF16), vn)
        for g in range(C_GROUP):
            og = jnp.zeros((QP, C_KV_WIDTH), F32)
            for k in range(C_KV_HEADS):
                hidx = k * C_GROUP + g
                og = jnp.where(lane == k, o[hidx * QP:(hidx + 1) * QP, :], og)
            o_ref[bi, :, g * C_KV_WIDTH:(g + 1) * C_KV_WIDTH] = og.astype(BF16)
        for src, shifted, dst in ((bk, ks_ref, nk_ref), (bv, vs_ref, nv_ref)):
            rolled = pltpu.roll(src, WINDOW - t, 0)
            dst[bi, 0:WINDOW - QP, :] = rolled[0:WINDOW - QP, :]
            dst[bi, WINDOW - QP:WINDOW, :] = jnp.where(row8 < QP - t, rolled[WINDOW - QP:WINDOW, :], shifted[bi])


def _swa_sample(q_rows, buf_k, buf_v, kn8, vn8, k_shift, v_shift, bias_buf, bias_new, sink_b, t):
    bd = q_rows.shape[0]
    rows = C_HEADS * QP
    kern = functools.partial(_swa_sample_kernel, t=t)
    g3 = lambda r, w: pl.BlockSpec((GB, r, w), lambda i: (i, 0, 0))
    g2 = lambda w: pl.BlockSpec((LANE, w), lambda i: (i, 0))
    return pl.pallas_call(
        kern, grid=(bd // GB,),
        in_specs=[g3(rows, C_KV_WIDTH), g3(WINDOW, C_KV_WIDTH), g3(WINDOW, C_KV_WIDTH), g2(C_KV_WIDTH), g2(C_KV_WIDTH),
                  g3(QP, C_KV_WIDTH), g3(QP, C_KV_WIDTH),
                  _const_spec((rows, LANE)), _const_spec((rows, LANE)), _const_spec((rows, LANE))],
        out_specs=[g3(QP, C_WIDTH), g3(WINDOW, C_KV_WIDTH), g3(WINDOW, C_KV_WIDTH)],
        out_shape=[jax.ShapeDtypeStruct((bd, QP, C_WIDTH), BF16),
                   jax.ShapeDtypeStruct((bd, WINDOW, C_KV_WIDTH), F32),
                   jax.ShapeDtypeStruct((bd, WINDOW, C_KV_WIDTH), F32)],
        compiler_params=_params(("parallel",)),
        name="swa_sample",
    )(q_rows, buf_k, buf_v, kn8, vn8, k_shift, v_shift, bias_buf, bias_new, sink_b)


def _pad_q(a, t):
    pad = [(0, 0), (0, QP - t)] + [(0, 0)] * (a.ndim - 2)
    return jnp.pad(a, pad)


def _sample_even_attention(se, bd, t, cache_a_k, cache_a_v, cache_a_idx, cache_b_ckv, cache_b_krope,
                           page_table, bias_rel):
    n_pages = page_table.shape[1]
    pps = _pages_per_step(n_pages, 16)
    nj = n_pages // pps
    r = lambda a: a.reshape(bd, t, -1)
    qi = r(se["qi"])
    qi_s = _pad_q(jnp.transpose(qi.reshape(bd, t, IDX_HEADS, IDX_DIM), (0, 2, 1, 3)).reshape(bd * IDX_HEADS, t, IDX_DIM), t)
    qi_s = qi_s.reshape(bd, IDX_HEADS * QP, IDX_DIM)
    wi = r(se["wi"])
    w_s = _pad_q(jnp.transpose(wi, (0, 2, 1)).reshape(bd * IDX_HEADS, t), t).reshape(bd, IDX_HEADS, QP, 1)
    w_s = jnp.broadcast_to(w_s, (bd, IDX_HEADS, QP, LANE))
    scores = _idx_scores_sample(page_table, qi_s, w_s, _slot_minor(cache_a_idx))
    qi8 = _pad_q(qi, t).reshape(bd * QP, IDX_WIDTH)
    wi8 = _pad_q(wi, t).reshape(bd * QP, IDX_HEADS)
    kin8 = _pad_q(r(se["ki2"]), t).reshape(bd * QP, 2 * IDX_DIM)
    mb_past, mb_new = _select_sample(scores.reshape(bd * QP, -1), qi8, wi8, kin8, t)
    head_of_lane = jnp.arange(A_WIDTH) // A_HEAD_DIM
    qa = _pad_q(r(se["qa"]), t)
    qa_s = jnp.where(head_of_lane[None, None, None, :] == jnp.arange(A_HEADS)[None, :, None, None],
                     qa[:, None, :, :], jnp.zeros((), qa.dtype)).reshape(bd, A_HEADS * QP, A_WIDTH)
    near = bias_rel[:, :QP, :]
    bias_pages = jnp.zeros((nj, A_HEADS * QP, pps * PAGE), F32)
    bias_pages = bias_pages.at[nj - 1, :, (pps - 1) * PAGE:].set(near[:, :, :PAGE].reshape(A_HEADS * QP, PAGE))
    bias_new = jnp.tile(near[:, :, PAGE:PAGE + QP], (1, 1, GB)).reshape(A_HEADS * QP, LANE)
    kn8 = _pad_q(r(se["ka_bf"]), t).reshape(bd * QP, A_WIDTH)
    vn8 = _pad_q(r(se["va_bf"]), t).reshape(bd * QP, A_WIDTH)
    oa = _attn_a_sample(page_table, qa_s, mb_past.reshape(bd, QP, -1), bias_pages, mb_new, bias_new, kn8, vn8,
                        _slot_minor(cache_a_k), _slot_minor(cache_a_v), pps)
    oa = oa[:, :t, :].reshape(bd * t, A_WIDTH)
    hq = lambda a, w: _pad_q(jnp.transpose(a.reshape(bd, t, B_HEADS, w), (0, 2, 1, 3)).reshape(bd * B_HEADS, t, w), t
                             ).reshape(bd, B_HEADS * QP, w)
    ql_s = hq(r(se["q_lat"]), B_KV_LORA)
    qr_s = hq(r(se["q_rope"]), B_ROPE)
    cn8 = _pad_q(r(se["ckv_bf"]), t).reshape(bd * QP, B_KV_LORA)
    rn8 = _pad_q(r(se["kr"]).astype(BF16), t).reshape(bd * QP, B_ROPE)
    ob = _mla_sample(page_table, ql_s, qr_s, cn8, rn8, cache_b_ckv, _slot_minor(cache_b_krope), t)
    ob = jnp.transpose(ob.reshape(bd, B_HEADS, QP, B_KV_LORA)[:, :, :t, :], (0, 2, 1, 3))
    return oa, ob.reshape(bd * t, B_HEADS * B_KV_LORA)


def _sample_odd_attention(q, k, kb, v, vb, buf_k, buf_v, sinks, bias_raw, bd, t):
    assert buf_k.shape[1] == WINDOW
    q5 = _pad_q(q.reshape(bd, t, C_GROUP, C_KV_HEADS, C_HEAD_DIM), t)
    q5 = jnp.transpose(q5, (0, 3, 2, 1, 4))
    kv_of_lane = jnp.arange(C_KV_WIDTH) // C_HEAD_DIM
    q_rows = jnp.where(kv_of_lane[None, None, None, None, :] == jnp.arange(C_KV_HEADS)[None, :, None, None, None],
                       jnp.tile(q5, (1, 1, 1, 1, C_KV_HEADS)), jnp.zeros((), q.dtype))
    q_rows = q_rows.reshape(bd, C_HEADS * QP, C_KV_WIDTH)
    near = bias_raw[:, :QP, :]
    bias_buf = near[:, :, :WINDOW].reshape(C_HEADS * QP, WINDOW)
    bias_new = jnp.tile(near[:, :, WINDOW:WINDOW + QP], (1, 1, GB)).reshape(C_HEADS * QP, LANE)
    sink_b = jnp.broadcast_to(jnp.repeat(sinks.astype(F32), QP)[:, None], (C_HEADS * QP, LANE))
    r = lambda a: a.reshape(bd, t, C_KV_WIDTH)
    kn8 = _pad_q(r(kb), t).reshape(bd * QP, C_KV_WIDTH)
    vn8 = _pad_q(r(vb), t).reshape(bd * QP, C_KV_WIDTH)
    shift = lambda a: jnp.pad(r(a), [(0, 0), (QP - t, 0), (0, 0)])
    o8, nk, nv = _swa_sample(q_rows, buf_k.reshape(bd, WINDOW, C_KV_WIDTH), buf_v.reshape(bd, WINDOW, C_KV_WIDTH),
                             kn8, vn8, shift(k), shift(v), bias_buf, bias_new, sink_b, t)
    o = o8[:, :t, :].reshape(bd * t, C_WIDTH)
    return o, nk.reshape(bd, WINDOW, C_KV_HEADS, C_HEAD_DIM), nv.reshape(bd, WINDOW, C_KV_HEADS, C_HEAD_DIM)


def _out_proj_kernel(o_ref, h_ref, w_ref, out_ref):
    out_ref[...] = h_ref[...] + _dot(o_ref[...], w_ref[...])


def _out_proj(o, h, w):
    n, d = h.shape
    tm = min(ROW_TILE, n)
    row = lambda wd: pl.BlockSpec((tm, wd), lambda i: (i, 0))
    return pl.pallas_call(
        _out_proj_kernel,
        grid=(n // tm,),
        in_specs=[row(o.shape[1]), row(d), _const_spec(w.shape)],
        out_specs=row(d),
        out_shape=jax.ShapeDtypeStruct((n, d), F32),
        compiler_params=_params(("parallel",)),
        name="out_proj",
    )(o, h, w)


def kernel(x_prompt, x_sample, cache_a_k, cache_a_v, cache_a_idx, cache_b_ckv, cache_b_krope, state_c_k, state_c_v, page_table, p_prompt, p_sample, rel_table, w_in_even, w_out_even, g_bq, w_buq, g_bkv, w_buk, w_buv, w_in_odd, w_out_odd, c_sinks, g_mix, g_ffn, w_ffn_gate, w_ffn_up, w_ffn_down, g_ple, w_ple_gate, w_ple_proj, g_final):
    b, s, d = x_prompt.shape
    bd, t, _ = x_sample.shape
    depth = g_mix.shape[0]
    assert depth == 2 and w_in_even.shape[0] == 1 and w_in_odd.shape[0] == 1
    assert s % 256 == 0 and rel_table.shape == (REL_BUCKETS, A_HEADS)
    assert t <= QP and bd % GB == 0
    past = page_table.shape[1] * PAGE

    we = _prep_even_weights(w_in_even[0], w_out_even[0], g_bq[0], w_buq[0], g_bkv[0], w_buk[0], w_buv[0])
    w_in_o, w_out_o = _prep_odd_weights(w_in_odd[0], w_out_odd[0])
    lws = [{"g_ffn": g_ffn[i].reshape(1, d), "g_ple": g_ple[i].reshape(1, d),
            "wg": w_ffn_gate[i].astype(BF16), "wu": w_ffn_up[i].astype(BF16), "wd": w_ffn_down[i].astype(BF16),
            "wpg": w_ple_gate[i].astype(BF16), "wpp": w_ple_proj[i].astype(BF16)} for i in range(depth)]
    gfin = g_final.reshape(1, d)
    bias_raw = _bias_window(rel_table)
    bias_rel = bias_raw - rel_table[REL_BUCKETS - 1].astype(F32)[:, None, None]
    cos_p, sin_p = _rope_tables(jnp.arange(s, dtype=jnp.int32), B_HEADS)
    cos_s, sin_s = _rope_tables(jnp.tile(past + jnp.arange(t, dtype=jnp.int32), bd), B_HEADS)

    hp = x_prompt.reshape(b * s, d)
    hs = x_sample.reshape(bd * t, d)
    g0 = g_mix[0].reshape(1, d)
    g1 = g_mix[1].reshape(1, d)

    pe = _even_proj(hp, g0, we, cos_p, sin_p)
    r3 = lambda a: a.reshape(b, s, a.shape[-1])
    oa = _attn_a_prompt(r3(pe["qi"]), r3(pe["wi"]), r3(pe["qa"]), r3(pe["ki2"]), r3(pe["ka_bf"]), r3(pe["va_bf"]),
                        bias_rel)
    obl = _mla_prompt(r3(pe["q_lat"]), r3(pe["q_rope"]), r3(pe["ckv_bf"]), r3(pe["kr_rep"]))
    hp = _even_merge(oa.reshape(b * s, A_WIDTH), obl.reshape(b * s, -1), hp, we)
    hp = _tail(hp, p_prompt[0].reshape(b * s, -1), lws[0], gfin, False)

    se = _even_proj(hs, g0, we, cos_s, sin_s)
    oa_s, obl_s = _sample_even_attention(se, bd, t, cache_a_k, cache_a_v, cache_a_idx, cache_b_ckv, cache_b_krope,
                                         page_table, bias_rel)
    hs = _even_merge(oa_s, obl_s, hs, we)
    hs = _tail(hs, p_sample[0].reshape(bd * t, -1), lws[0], gfin, False)

    q, k, kb, v, vb = _odd_proj(hp, g1, w_in_o)
    hp = _swa_prompt(q.reshape(b, s, -1), kb.reshape(b, s, -1), vb.reshape(b, s, -1), bias_raw, c_sinks[0],
                     hp.reshape(b, s, d), w_out_o).reshape(b * s, d)
    y_prompt = _tail(hp, p_prompt[1].reshape(b * s, -1), lws[1], gfin, True)
    wp = min(WINDOW, s)
    pc_k = k.reshape(b, s, C_KV_HEADS, C_HEAD_DIM)[:, s - wp:]
    pc_v = v.reshape(b, s, C_KV_HEADS, C_HEAD_DIM)[:, s - wp:]

    qs, ks, ksb, vs, vsb = _odd_proj(hs, g1, w_in_o)
    os_, sc_k, sc_v = _sample_odd_attention(qs, ks, ksb, vs, vsb, state_c_k[0], state_c_v[0], c_sinks[0], bias_raw,
                                            bd, t)
    hs = _out_proj(os_, hs, w_out_o)
    y_sample = _tail(hs, p_sample[1].reshape(bd * t, -1), lws[1], gfin, True)

    hd = (A_HEADS, A_HEAD_DIM)
    return (y_prompt.reshape(b, s, d), y_sample.reshape(bd, t, d),
            pe["ka"].reshape(1, b, s, *hd), pe["va"].reshape(1, b, s, *hd), pe["ki"].reshape(1, b, s, IDX_DIM),
            pe["ckv"].reshape(1, b, s, B_KV_LORA), pe["kr"].reshape(1, b, s, B_ROPE),
            pc_k[None], pc_v[None],
            se["ka"].reshape(1, bd, t, *hd), se["va"].reshape(1, bd, t, *hd), se["ki"].reshape(1, bd, t, IDX_DIM),
            se["ckv"].reshape(1, bd, t, B_KV_LORA), se["kr"].reshape(1, bd, t, B_ROPE),
            sc_k[None], sc_v[None])
```
